```python
import math
import jax, jax.numpy as jnp
from jax import lax
import numpy as np

D_MODEL = 1024
BATCH = 2
SEQ = 8192
DEPTH = 4
DEC_BATCH = 32
DEC_SEQ = 1
PAST_LEN = 8192
PAGE_SIZE = 128

N_SSM_LAYERS = (DEPTH + 1) // 2
N_ATTN_LAYERS = DEPTH // 2
S5_WIDTH = D_MODEL // 2
S5_GROUP = 16
S5_GROUPS = S5_WIDTH // S5_GROUP
S5_STATE = 64
S5_DT_MIN = 1e-3
S5_DT_MAX = 1e-1
SSD_WIDTH = D_MODEL
SSD_HEAD_DIM = 64
SSD_HEADS = SSD_WIDTH // SSD_HEAD_DIM
SSD_GROUPS = 2
SSD_STATE = 128
SSD_CONV = 4
SSD_CHUNK = 128
SSD_CONV_DIM = SSD_WIDTH + 2 * SSD_GROUPS * SSD_STATE
SSM_IN_DIM = S5_WIDTH + SSD_WIDTH + SSD_CONV_DIM + SSD_HEADS
SSM_MIX_DIM = S5_WIDTH + SSD_WIDTH
DA_HEADS = 8
DA_HEAD_DIM = 64
DA_Q_DIM = DA_HEADS * 2 * DA_HEAD_DIM
ROT_DIM = DA_HEAD_DIM // 4
ROPE_THETA = 500000.0
ATTN_BLOCK = 128
D_FF = 4 * D_MODEL
PLE_DIM = 256
DN_ALPHA = (2 * DEPTH) ** 0.25
DN_BETA = (8 * DEPTH) ** -0.25
LN_EPS = 1e-5
RMS_EPS = 1e-5

kernel_name = 'hybrid_s5_ssd_diffattn_step'


def layer_norm(x, g, b):
    xf = x.astype(jnp.float32)
    mu = jnp.mean(xf, -1, keepdims=True)
    var = jnp.mean(jnp.square(xf - mu), -1, keepdims=True)
    return ((xf - mu) * lax.rsqrt(var + LN_EPS) * g.astype(jnp.float32) + b.astype(jnp.float32)).astype(x.dtype)


def rms_norm(x, g):
    xf = x.astype(jnp.float32)
    return (xf * lax.rsqrt(jnp.mean(xf * xf, -1, keepdims=True) + RMS_EPS) * g.astype(jnp.float32)).astype(x.dtype)


def _complex_combine(e1, e2):
    ar1, ai1, br1, bi1 = e1
    ar2, ai2, br2, bi2 = e2
    return (ar1 * ar2 - ai1 * ai2, ar1 * ai2 + ai1 * ar2,
            ar2 * br1 - ai2 * bi1 + br2, ar2 * bi1 + ai2 * br1 + bi2)


def s5_mixer(u, h0_re, h0_im, lam_re, lam_im, log_step, b_re, b_im, c_re, c_im, d_skip, w_glu, b_glu):
    f32 = jnp.float32
    bsz, seq, _ = u.shape
    uf = u.astype(f32).reshape(bsz, seq, S5_GROUPS, S5_GROUP)
    step = jnp.exp(log_step.astype(f32))[:, None]
    lr = lam_re.astype(f32)
    li = lam_im.astype(f32)
    mag = jnp.exp(lr * step)
    a_re = mag * jnp.cos(li * step)
    a_im = mag * jnp.sin(li * step)
    den = lr * lr + li * li
    coef_re = ((a_re - 1.0) * lr + a_im * li) / den
    coef_im = (a_im * lr - (a_re - 1.0) * li) / den
    br = b_re.astype(f32)
    bi = b_im.astype(f32)
    bb_re = coef_re[..., None] * br - coef_im[..., None] * bi
    bb_im = coef_re[..., None] * bi + coef_im[..., None] * br
    bu_re = jnp.einsum('blgh,gph->blgp', uf, bb_re)
    bu_im = jnp.einsum('blgh,gph->blgp', uf, bb_im)
    h0r = h0_re.astype(f32)
    h0i = h0_im.astype(f32)
    bu_re = bu_re.at[:, 0].add(a_re * h0r - a_im * h0i)
    bu_im = bu_im.at[:, 0].add(a_re * h0i + a_im * h0r)
    ar = jnp.broadcast_to(a_re, bu_re.shape)
    ai = jnp.broadcast_to(a_im, bu_im.shape)
    _, _, hr, hi = lax.associative_scan(_complex_combine, (ar, ai, bu_re, bu_im), axis=1)
    y = (jnp.einsum('blgp,ghp->blgh', hr, c_re.astype(f32))
         - jnp.einsum('blgp,ghp->blgh', hi, c_im.astype(f32))
         + d_skip.astype(f32).reshape(S5_GROUPS, S5_GROUP) * uf)
    y = jax.nn.gelu(y.reshape(bsz, seq, S5_WIDTH))
    out = y * jax.nn.sigmoid(y @ w_glu.astype(f32) + b_glu.astype(f32))
    return out.astype(u.dtype), hr[:, -1], hi[:, -1]


def causal_conv(xbc, conv_state, w, b):
    seq = xbc.shape[1]
    full = jnp.concatenate([conv_state.astype(xbc.dtype), xbc], axis=1)
    out = b
    for t in range(SSD_CONV):
        out = out + full[:, t:t + seq] * w[t]
    return jax.nn.silu(out), full[:, full.shape[1] - (SSD_CONV - 1):]


def segsum(x):
    cs = jnp.cumsum(x, axis=-1)
    t = x.shape[-1]
    mask = jnp.tril(jnp.ones((t, t), dtype=bool))
    return jnp.where(mask, cs[..., :, None] - cs[..., None, :], -jnp.inf)


def ssd_scan(x, dt, a, bm, cm, h0):
    bsz, seq = x.shape[:2]
    q = SSD_CHUNK if seq % SSD_CHUNK == 0 else seq
    nc = seq // q
    hg = SSD_HEADS // SSD_GROUPS
    xr = (x * dt[..., None]).reshape(bsz, nc, q, SSD_GROUPS, hg, SSD_HEAD_DIM)
    av = (dt * a).reshape(bsz, nc, q, SSD_GROUPS, hg).transpose(0, 3, 4, 1, 2)
    bc = bm.reshape(bsz, nc, q, SSD_GROUPS, SSD_STATE)
    cc = cm.reshape(bsz, nc, q, SSD_GROUPS, SSD_STATE)
    a_cs = jnp.cumsum(av, axis=-1)
    lmat = jnp.exp(segsum(av))
    cb = jnp.einsum('bclgn,bcsgn->bgcls', cc, bc)
    y_diag = jnp.einsum('bgkcls,bcsgkp->bclgkp', cb[:, :, None] * lmat, xr)
    decay_states = jnp.exp(a_cs[..., -1:] - a_cs)
    states = jnp.einsum('bclgn,bgkcl,bclgkp->bcgkpn', bc, decay_states, xr)
    h0g = h0.reshape(bsz, 1, SSD_GROUPS, hg, SSD_HEAD_DIM, SSD_STATE)
    states = jnp.concatenate([h0g, states], axis=1)
    chunk_tot = jnp.pad(a_cs[..., -1], ((0, 0), (0, 0), (0, 0), (1, 0)))
    decay_chunk = jnp.exp(segsum(chunk_tot))
    new_states = jnp.einsum('bgkzc,bcgkpn->bzgkpn', decay_chunk, states)
    y_off = jnp.einsum('bclgn,bcgkpn,bgkcl->bclgkp', cc, new_states[:, :-1], jnp.exp(a_cs))
    y = (y_diag + y_off).reshape(bsz, seq, SSD_HEADS, SSD_HEAD_DIM)
    return y, new_states[:, -1].reshape(bsz, SSD_HEADS, SSD_HEAD_DIM, SSD_STATE)


def ssd_mixer(z, xbc, dt_raw, conv_state, h0, conv_w, conv_b, dt_bias, a_log, d_skip, norm_w):
    f32 = jnp.float32
    bsz, seq, _ = z.shape
    xbc, new_conv = causal_conv(xbc, conv_state, conv_w, conv_b)
    xs, bm, cm = jnp.split(xbc, [SSD_WIDTH, SSD_WIDTH + SSD_GROUPS * SSD_STATE], axis=-1)
    xs = xs.astype(f32).reshape(bsz, seq, SSD_HEADS, SSD_HEAD_DIM)
    bm = bm.astype(f32).reshape(bsz, seq, SSD_GROUPS, SSD_STATE)
    cm = cm.astype(f32).reshape(bsz, seq, SSD_GROUPS, SSD_STATE)
    dt = jax.nn.softplus(dt_raw.astype(f32) + dt_bias.astype(f32))
    a = -jnp.exp(a_log.astype(f32))
    y, h_last = ssd_scan(xs, dt, a, bm, cm, h0.astype(f32))
    y = (y + d_skip.astype(f32)[:, None] * xs).reshape(bsz, seq, SSD_WIDTH)
    y = rms_norm(y * jax.nn.silu(z.astype(f32)), norm_w)
    return y.astype(z.dtype), new_conv, h_last


def ssm_block(x, h_re, h_im, h_ssd, conv_st, w_in, lam_re, lam_im, log_step, b_re, b_im, c_re, c_im,
              s5_d, w_glu, b_glu, conv_w, conv_b, dt_bias, a_log, ssd_d, norm_w, w_out):
    proj = x @ w_in
    u, z, xbc, dt_raw = jnp.split(proj, [S5_WIDTH, S5_WIDTH + SSD_WIDTH, S5_WIDTH + SSD_WIDTH + SSD_CONV_DIM], axis=-1)
    ya, hr, hi = s5_mixer(u, h_re, h_im, lam_re, lam_im, log_step, b_re, b_im, c_re, c_im, s5_d, w_glu, b_glu)
    yb, new_conv, h_last = ssd_mixer(z, xbc, dt_raw, conv_st, h_ssd, conv_w, conv_b, dt_bias, a_log, ssd_d, norm_w)
    out = jnp.concatenate([ya, yb], axis=-1) @ w_out
    return out, hr, hi, h_last, new_conv


def rope(x, pos):
    f32 = jnp.float32
    inv = ROPE_THETA ** (-jnp.arange(0, ROT_DIM, 2, dtype=f32) / ROT_DIM)
    ang = pos.astype(f32)[:, None] * inv
    cos = jnp.cos(ang)[:, None, None, :]
    sin = jnp.sin(ang)[:, None, None, :]
    xr = x[..., :ROT_DIM].astype(f32)
    x1, x2 = xr[..., :ROT_DIM // 2], xr[..., ROT_DIM // 2:]
    rot = jnp.concatenate([x1 * cos - x2 * sin, x2 * cos + x1 * sin], axis=-1)
    return jnp.concatenate([rot.astype(x.dtype), x[..., ROT_DIM:]], axis=-1)


def diff_attn_causal(q, k, v, lam):
    bsz, seq = q.shape[:2]
    nb = seq // ATTN_BLOCK
    scale = DA_HEAD_DIM ** -0.5
    qb = q.reshape(bsz, nb, ATTN_BLOCK, DA_HEADS, 2, DA_HEAD_DIM).swapaxes(0, 1)
    kpos = jnp.arange(seq)

    def block(args):
        qi, bi = args
        qpos = bi * ATTN_BLOCK + jnp.arange(ATTN_BLOCK)
        mask = kpos[None, :] <= qpos[:, None]
        s = jnp.einsum('bqhcd,bkhcd->bhcqk', qi, k).astype(jnp.float32) * scale
        p = jax.nn.softmax(jnp.where(mask, s, -jnp.inf), axis=-1)
        w = p[:, :, 0] - lam * p[:, :, 1]
        return jnp.einsum('bhqk,bkhe->bqhe', w.astype(v.dtype), v)

    out = lax.map(block, (qb, jnp.arange(nb)))
    return out.swapaxes(0, 1).reshape(bsz, seq, DA_HEADS, 2 * DA_HEAD_DIM)


def diff_attn_with_past(q, k, v, k_past, v_past, lam):
    seq = q.shape[1]
    t_past = k_past.shape[1]
    scale = DA_HEAD_DIM ** -0.5
    s_past = jnp.einsum('bqhcd,bkhcd->bhcqk', q, k_past).astype(jnp.float32) * scale
    s_new = jnp.einsum('bqhcd,bkhcd->bhcqk', q, k).astype(jnp.float32) * scale
    mask = jnp.tril(jnp.ones((seq, seq), dtype=bool))
    s = jnp.concatenate([s_past, jnp.where(mask, s_new, -jnp.inf)], axis=-1)
    p = jax.nn.softmax(s, axis=-1)
    w = (p[:, :, 0] - lam * p[:, :, 1]).astype(v.dtype)
    return (jnp.einsum('bhqk,bkhe->bqhe', w[..., :t_past], v_past)
            + jnp.einsum('bhqk,bkhe->bqhe', w[..., t_past:], v))


def attn_block(x, pos, past_k, past_v, lam_init, w_qkv, lq1, lk1, lq2, lk2, subln_w, w_out):
    f32 = jnp.float32
    bsz, seq, _ = x.shape
    q, k, v = jnp.split(x @ w_qkv, 3, axis=-1)
    q = rope(q.reshape(bsz, seq, DA_HEADS, 2, DA_HEAD_DIM), pos)
    k = rope(k.reshape(bsz, seq, DA_HEADS, 2, DA_HEAD_DIM), pos)
    v = v.reshape(bsz, seq, DA_HEADS, 2 * DA_HEAD_DIM)
    lam = (jnp.exp(jnp.sum(lq1.astype(f32) * lk1.astype(f32)))
           - jnp.exp(jnp.sum(lq2.astype(f32) * lk2.astype(f32))) + lam_init)
    if past_k is None:
        o = diff_attn_causal(q, k, v, lam)
    else:
        o = diff_attn_with_past(q, k, v, past_k, past_v, lam)
    o = rms_norm(o, subln_w) * (1.0 - lam_init)
    out = o.reshape(bsz, seq, DA_Q_DIM) @ w_out
    return out, k.reshape(bsz, seq, DA_HEADS, 2 * DA_HEAD_DIM), v


def finish_layer(x, mix, p, g1, b1, g2, b2, w1, w2, w_proj, w_gate):
    x = layer_norm(DN_ALPHA * x + mix.astype(x.dtype), g1, b1)
    h = jnp.square(jax.nn.relu(x @ w1)) @ w2
    x = layer_norm(DN_ALPHA * x + h, g2, b2)
    return x + jax.nn.sigmoid(x @ w_gate) * (p @ w_proj)


def setup_inputs(seed: int = 0) -> dict:
    key = jax.random.key(seed)
    ks = iter(jax.random.split(key, 64))
    f32 = jnp.float32
    nrm = lambda shape, s=1.0: jax.random.normal(next(ks), shape, f32) * s
    n_pages = PAST_LEN // PAGE_SIZE
    in_use = DEC_BATCH * n_pages
    n_phys = in_use + max(1, in_use // 4)
    page_table = jax.random.permutation(next(ks), n_phys)[:in_use].reshape(DEC_BATCH, n_pages).astype(jnp.int32)
    dt_ssd = jnp.exp(jax.random.uniform(next(ks), (N_SSM_LAYERS, SSD_HEADS), f32, math.log(1e-3), math.log(1e-1)))
    return {
        'x_prompt': nrm((BATCH, SEQ, D_MODEL)),
        'x_sample': nrm((DEC_BATCH, DEC_SEQ, D_MODEL)),
        'state_s5_re': nrm((N_SSM_LAYERS, DEC_BATCH, S5_GROUPS, S5_STATE)),
        'state_s5_im': nrm((N_SSM_LAYERS, DEC_BATCH, S5_GROUPS, S5_STATE)),
        'state_ssd': nrm((N_SSM_LAYERS, DEC_BATCH, SSD_HEADS, SSD_HEAD_DIM, SSD_STATE), 0.5),
        'state_conv': nrm((N_SSM_LAYERS, DEC_BATCH, SSD_CONV - 1, SSD_CONV_DIM)),
        'cache_k': nrm((N_ATTN_LAYERS, n_phys, PAGE_SIZE, DA_HEADS, 2 * DA_HEAD_DIM)),
        'cache_v': nrm((N_ATTN_LAYERS, n_phys, PAGE_SIZE, DA_HEADS, 2 * DA_HEAD_DIM)),
        'page_table': page_table,
        'p_prompt': nrm((DEPTH, BATCH, SEQ, PLE_DIM)),
        'p_sample': nrm((DEPTH, DEC_BATCH, DEC_SEQ, PLE_DIM)),
        'ln1_g': 1.0 + nrm((DEPTH, D_MODEL), 0.01),
        'ln1_b': nrm((DEPTH, D_MODEL), 0.01),
        'ln2_g': 1.0 + nrm((DEPTH, D_MODEL), 0.01),
        'ln2_b': nrm((DEPTH, D_MODEL), 0.01),
        'ssm_w_in': nrm((N_SSM_LAYERS, D_MODEL, SSM_IN_DIM), D_MODEL ** -0.5),
        's5_lam_re': -0.5 + nrm((N_SSM_LAYERS, S5_GROUPS, S5_STATE), 0.01),
        's5_lam_im': jnp.pi * jnp.arange(S5_STATE, dtype=f32) + nrm((N_SSM_LAYERS, S5_GROUPS, S5_STATE), 0.01),
        's5_log_step': jax.random.uniform(next(ks), (N_SSM_LAYERS, S5_GROUPS), f32, math.log(S5_DT_MIN), math.log(S5_DT_MAX)),
        's5_b_re': nrm((N_SSM_LAYERS, S5_GROUPS, S5_STATE, S5_GROUP), (2 * S5_GROUP) ** -0.5),
        's5_b_im': nrm((N_SSM_LAYERS, S5_GROUPS, S5_STATE, S5_GROUP), (2 * S5_GROUP) ** -0.5),
        's5_c_re': nrm((N_SSM_LAYERS, S5_GROUPS, S5_GROUP, S5_STATE), S5_STATE ** -0.5),
        's5_c_im': nrm((N_SSM_LAYERS, S5_GROUPS, S5_GROUP, S5_STATE), S5_STATE ** -0.5),
        's5_d': nrm((N_SSM_LAYERS, S5_WIDTH)),
        's5_w_glu': nrm((N_SSM_LAYERS, S5_WIDTH, S5_WIDTH), S5_WIDTH ** -0.5),
        's5_b_glu': nrm((N_SSM_LAYERS, S5_WIDTH), 0.01),
        'ssd_conv_w': nrm((N_SSM_LAYERS, SSD_CONV, SSD_CONV_DIM), SSD_CONV ** -0.5),
        'ssd_conv_b': nrm((N_SSM_LAYERS, SSD_CONV_DIM), 0.01),
        'ssd_dt_bias': dt_ssd + jnp.log(-jnp.expm1(-dt_ssd)),
        'ssd_a_log': jnp.log(jax.random.uniform(next(ks), (N_SSM_LAYERS, SSD_HEADS), f32, 1.0, 16.0)),
        'ssd_d': 1.0 + nrm((N_SSM_LAYERS, SSD_HEADS), 0.01),
        'ssd_norm_w': 1.0 + nrm((N_SSM_LAYERS, SSD_WIDTH), 0.01),
        'ssm_w_out': nrm((N_SSM_LAYERS, SSM_MIX_DIM, D_MODEL), SSM_MIX_DIM ** -0.5 * DN_BETA),
        'attn_w_qkv': nrm((N_ATTN_LAYERS, D_MODEL, 3 * DA_Q_DIM), D_MODEL ** -0.5),
        'attn_lq1': nrm((N_ATTN_LAYERS, DA_HEAD_DIM), 0.1),
        'attn_lk1': nrm((N_ATTN_LAYERS, DA_HEAD_DIM), 0.1),
        'attn_lq2': nrm((N_ATTN_LAYERS, DA_HEAD_DIM), 0.1),
        'attn_lk2': nrm((N_ATTN_LAYERS, DA_HEAD_DIM), 0.1),
        'attn_subln_w': 1.0 + nrm((N_ATTN_LAYERS, 2 * DA_HEAD_DIM), 0.01),
        'attn_w_out': nrm((N_ATTN_LAYERS, DA_Q_DIM, D_MODEL), DA_Q_DIM ** -0.5 * DN_BETA),
        'mlp_w1': nrm((DEPTH, D_MODEL, D_FF), D_MODEL ** -0.5),
        'mlp_w2': nrm((DEPTH, D_FF, D_MODEL), D_FF ** -0.5 * DN_BETA),
        'ple_w_proj': nrm((DEPTH, PLE_DIM, D_MODEL), PLE_DIM ** -0.5),
        'ple_w_gate': nrm((DEPTH, D_MODEL, D_MODEL), D_MODEL ** -0.5),
    }


def reference(x_prompt, x_sample, state_s5_re, state_s5_im, state_ssd, state_conv, cache_k, cache_v,
              page_table, p_prompt, p_sample, ln1_g, ln1_b, ln2_g, ln2_b, ssm_w_in, s5_lam_re, s5_lam_im,
              s5_log_step, s5_b_re, s5_b_im, s5_c_re, s5_c_im, s5_d, s5_w_glu, s5_b_glu, ssd_conv_w,
              ssd_conv_b, ssd_dt_bias, ssd_a_log, ssd_d, ssd_norm_w, ssm_w_out, attn_w_qkv, attn_lq1,
              attn_lk1, attn_lq2, attn_lk2, attn_subln_w, attn_w_out, mlp_w1, mlp_w2, ple_w_proj, ple_w_gate):
    f32 = jnp.float32
    bp, seq_p, _ = x_prompt.shape
    bs, seq_s, _ = x_sample.shape
    past_len = page_table.shape[1] * cache_k.shape[2]
    pos_p = jnp.arange(seq_p)
    pos_s = past_len + jnp.arange(seq_s)
    xp, xs = x_prompt, x_sample
    s5re_p, s5im_p, s5re_s, s5im_s = [], [], [], []
    ssd_p, ssd_s, conv_p, conv_s = [], [], [], []
    k_p, v_p, k_s, v_s = [], [], [], []
    for i in range(DEPTH):
        j = i // 2
        if i % 2 == 0:
            prm = (ssm_w_in[j], s5_lam_re[j], s5_lam_im[j], s5_log_step[j], s5_b_re[j], s5_b_im[j],
                   s5_c_re[j], s5_c_im[j], s5_d[j], s5_w_glu[j], s5_b_glu[j], ssd_conv_w[j], ssd_conv_b[j],
                   ssd_dt_bias[j], ssd_a_log[j], ssd_d[j], ssd_norm_w[j], ssm_w_out[j])
            z_s5 = jnp.zeros((bp, S5_GROUPS, S5_STATE), f32)
            z_ssd = jnp.zeros((bp, SSD_HEADS, SSD_HEAD_DIM, SSD_STATE), f32)
            z_conv = jnp.zeros((bp, SSD_CONV - 1, SSD_CONV_DIM), xp.dtype)
            mp, hr, hi, hl, cv = ssm_block(xp, z_s5, z_s5, z_ssd, z_conv, *prm)
            ms, hr2, hi2, hl2, cv2 = ssm_block(xs, state_s5_re[j], state_s5_im[j], state_ssd[j], state_conv[j], *prm)
            s5re_p.append(hr); s5im_p.append(hi); s5re_s.append(hr2); s5im_s.append(hi2)
            ssd_p.append(hl); ssd_s.append(hl2); conv_p.append(cv); conv_s.append(cv2)
        else:
            lam_init = 0.8 - 0.6 * math.exp(-0.3 * i)
            prm = (attn_w_qkv[j], attn_lq1[j], attn_lk1[j], attn_lq2[j], attn_lk2[j], attn_subln_w[j], attn_w_out[j])
            mp, kp, vp = attn_block(xp, pos_p, None, None, lam_init, *prm)
            past_k = cache_k[j, page_table].reshape(bs, past_len, DA_HEADS, 2, DA_HEAD_DIM)
            past_v = cache_v[j, page_table].reshape(bs, past_len, DA_HEADS, 2 * DA_HEAD_DIM)
            ms, ks_, vs_ = attn_block(xs, pos_s, past_k, past_v, lam_init, *prm)
            k_p.append(kp); v_p.append(vp); k_s.append(ks_); v_s.append(vs_)
        xp = finish_layer(xp, mp, p_prompt[i], ln1_g[i], ln1_b[i], ln2_g[i], ln2_b[i], mlp_w1[i], mlp_w2[i], ple_w_proj[i], ple_w_gate[i])
        xs = finish_layer(xs, ms, p_sample[i], ln1_g[i], ln1_b[i], ln2_g[i], ln2_b[i], mlp_w1[i], mlp_w2[i], ple_w_proj[i], ple_w_gate[i])
    s5_re_prompt = jnp.stack(s5re_p)
    s5_im_prompt = jnp.stack(s5im_p)
    s5_re_sample = jnp.stack(s5re_s)
    s5_im_sample = jnp.stack(s5im_s)
    ssd_prompt = jnp.stack(ssd_p)
    ssd_sample = jnp.stack(ssd_s)
    conv_prompt = jnp.stack(conv_p)
    conv_sample = jnp.stack(conv_s)
    k_prompt = jnp.stack(k_p)
    v_prompt = jnp.stack(v_p)
    k_sample = jnp.stack(k_s)
    v_sample = jnp.stack(v_s)
    return (xp, xs, s5_re_prompt, s5_im_prompt, s5_re_sample, s5_im_sample, ssd_prompt, ssd_sample,
            conv_prompt, conv_sample, k_prompt, v_prompt, k_sample, v_sample)
```

```python
import functools
import math

import jax
import jax.numpy as jnp
from jax import lax
from jax.experimental import pallas as pl
from jax.experimental.pallas import tpu as pltpu

F32 = jnp.float32
BF16 = jnp.bfloat16
HIGHEST = lax.Precision.HIGHEST

S5_GROUP = 16
SSD_HEAD_DIM = 64
SSD_GROUPS = 2
DA_HEAD_DIM = 64
ROT_DIM = DA_HEAD_DIM // 4
ROPE_THETA = 500000.0
LN_EPS = 1e-5
RMS_EPS = 1e-5

LANES = 128
SUBLANES = 8
VMEM_LIMIT_BYTES = 56 * 1024 * 1024

ROW_TILE = 512
S5_SEGMENTS = SUBLANES
S5_SEG_LEN = 64
S5_LANE_CHUNK = 512
SSD_CHUNK = 256
ATTN_TQ = 512
ATTN_TK = 512
DEC_PAGES_PER_STEP = 4
NEG_BIG = -1e30


def _resident(shape):
    zeros = (0,) * len(shape)
    return pl.BlockSpec(shape, lambda *_: zeros, pipeline_mode=pl.Buffered(1))


def _params(*sem):
    return pltpu.CompilerParams(dimension_semantics=sem, vmem_limit_bytes=VMEM_LIMIT_BYTES)


def _dot(a, b):
    return jnp.dot(a.astype(BF16), b.astype(BF16), preferred_element_type=F32)


def _dot_nt(a, b):
    return lax.dot_general(a.astype(BF16), b.astype(BF16), (((1,), (1,)), ((), ())),
                           preferred_element_type=F32)


def _dot_exact(a, b):
    return jnp.dot(a, b, precision=HIGHEST, preferred_element_type=F32)


def _layer_norm(x, g, b):
    mu = jnp.mean(x, -1, keepdims=True)
    xc = x - mu
    var = jnp.mean(xc * xc, -1, keepdims=True)
    return xc * lax.rsqrt(var + LN_EPS) * g + b


def _rms_norm(x, g):
    return x * lax.rsqrt(jnp.mean(x * x, -1, keepdims=True) + RMS_EPS) * g


def _row_tile(m):
    return ROW_TILE if m % ROW_TILE == 0 else m


def _proj_kernel(x_ref, w_ref, *o_refs, splits):
    xb = x_ref[...].astype(BF16)
    off = 0
    for o_ref, n in zip(o_refs, splits):
        o_ref[...] = jnp.dot(xb, w_ref[:, off:off + n], preferred_element_type=F32).astype(o_ref.dtype)
        off += n


def _proj(x, w, splits):
    m, k = x.shape
    tm = _row_tile(m)
    return pl.pallas_call(
        functools.partial(_proj_kernel, splits=splits),
        grid=(m // tm,),
        in_specs=[pl.BlockSpec((tm, k), lambda i: (i, 0)), _resident(w.shape)],
        out_specs=[pl.BlockSpec((tm, n), lambda i: (i, 0)) for n in splits],
        out_shape=[jax.ShapeDtypeStruct((m, n), F32) for n in splits],
        compiler_params=_params("parallel"),
    )(x, w)


def _rope(t, ca, sa, sb):
    half = ROT_DIM // 2
    outs = []
    for h in range(t.shape[1] // LANES):
        c = t[:, h * LANES:(h + 1) * LANES]
        up = pltpu.roll(c, LANES - half, 1)
        dn = pltpu.roll(c, half, 1)
        outs.append(c * ca + up * sa + dn * sb)
    return jnp.concatenate(outs, axis=1)


def _qkv_kernel(x_ref, w_ref, ca_ref, sa_ref, sb_ref, k_ref, v_ref, qb_ref, kb_ref, vb_ref, *, width, scale):
    xb = x_ref[...].astype(BF16)
    ca, sa, sb = ca_ref[...], sa_ref[...], sb_ref[...]
    q = jnp.dot(xb, w_ref[:, 0:width], preferred_element_type=F32)
    qb_ref[...] = (_rope(q, ca, sa, sb) * scale).astype(qb_ref.dtype)
    k = _rope(jnp.dot(xb, w_ref[:, width:2 * width], preferred_element_type=F32), ca, sa, sb)
    k_ref[...] = k
    kb_ref[...] = k.astype(BF16)
    v = jnp.dot(xb, w_ref[:, 2 * width:3 * width], preferred_element_type=F32)
    v_ref[...] = v
    vb_ref[...] = v.astype(BF16)


def _rope_tables(pos):
    half = ROT_DIM // 2
    inv = ROPE_THETA ** (-jnp.arange(0, ROT_DIM, 2, dtype=F32) / ROT_DIM)
    ang = pos.astype(F32)[:, None] * inv
    cos, sin = jnp.cos(ang), jnp.sin(ang)
    n = pos.shape[0]
    pad = jnp.zeros((n, DA_HEAD_DIM - ROT_DIM), F32)
    zero = jnp.zeros((n, half), F32)
    ca = jnp.concatenate([cos, cos, pad + 1.0], axis=1)
    sa = jnp.concatenate([-sin, zero, pad], axis=1)
    sb = jnp.concatenate([zero, sin, pad], axis=1)
    rep = LANES // DA_HEAD_DIM
    return jnp.tile(ca, (1, rep)), jnp.tile(sa, (1, rep)), jnp.tile(sb, (1, rep))


def _qkv(x, w, tables, q_dtype):
    m, d = x.shape
    width = w.shape[1] // 3
    tm = _row_tile(m)
    nt = tables[0].shape[0] // tm
    row = lambda i: (i, 0)
    tab = pl.BlockSpec((tm, LANES), lambda i: (i % nt, 0))
    out = pl.BlockSpec((tm, width), row)
    return pl.pallas_call(
        functools.partial(_qkv_kernel, width=width, scale=DA_HEAD_DIM ** -0.5),
        grid=(m // tm,),
        in_specs=[pl.BlockSpec((tm, d), row), _resident(w.shape), tab, tab, tab],
        out_specs=[out] * 5,
        out_shape=[jax.ShapeDtypeStruct((m, width), F32)] * 2
        + [jax.ShapeDtypeStruct((m, width), q_dtype)] + [jax.ShapeDtypeStruct((m, width), BF16)] * 2,
        compiler_params=_params("parallel"),
    )(x, w, *tables)


def _finish_kernel(*refs, n_mix, alpha, ff_chunk):
    x_ref = refs[0]
    y_refs = refs[1:1 + n_mix]
    wo_refs = refs[1 + n_mix:1 + 2 * n_mix]
    (p_ref, g1_ref, b1_ref, g2_ref, b2_ref, w1_ref, w2_ref, wp_ref, wg_ref, o_ref) = refs[1 + 2 * n_mix:]
    mix = _dot(y_refs[0][...], wo_refs[0][...])
    for y_ref, wo_ref in zip(y_refs[1:], wo_refs[1:]):
        mix = mix + _dot(y_ref[...], wo_ref[...])
    x1 = _layer_norm(alpha * x_ref[...] + mix, g1_ref[...], b1_ref[...])
    x1b = x1.astype(BF16)
    h = None
    for c in range(0, w1_ref.shape[1], ff_chunk):
        a = jnp.maximum(jnp.dot(x1b, w1_ref[:, c:c + ff_chunk], preferred_element_type=F32), 0.0)
        part = jnp.dot((a * a).astype(BF16), w2_ref[c:c + ff_chunk, :], preferred_element_type=F32)
        h = part if h is None else h + part
    x2 = _layer_norm(alpha * x1 + h, g2_ref[...], b2_ref[...])
    gate = jax.nn.sigmoid(_dot(x2, wg_ref[...]))
    o_ref[...] = x2 + gate * _dot(p_ref[...], wp_ref[...])


def _finish(x, mixes, w_outs, p, g1, b1, g2, b2, w1, w2, wp, wg, alpha):
    m, d = x.shape
    tm = _row_tile(m)
    row = lambda i: (i, 0)
    vec = lambda a: a.reshape(1, d)
    in_specs = ([pl.BlockSpec((tm, d), row)]
                + [pl.BlockSpec((tm, y.shape[1]), row) for y in mixes]
                + [_resident(w.shape) for w in w_outs]
                + [pl.BlockSpec((tm, p.shape[1]), row)]
                + [_resident((1, d))] * 4
                + [_resident(w.shape) for w in (w1, w2, wp, wg)])
    return pl.pallas_call(
        functools.partial(_finish_kernel, n_mix=len(mixes), alpha=alpha, ff_chunk=1024),
        grid=(m // tm,),
        in_specs=in_specs,
        out_specs=pl.BlockSpec((tm, d), row),
        out_shape=jax.ShapeDtypeStruct((m, d), F32),
        compiler_params=_params("parallel"),
    )(x, *mixes, *w_outs, p, vec(g1), vec(b1), vec(g2), vec(b2), w1, w2, wp, wg)


def _s5_prep_kernel(lr_ref, li_ref, ls_ref, br_ref, bi_ref, ar_ref, ai_ref, bbr_ref, bbi_ref):
    lr, li = lr_ref[...], li_ref[...]
    step = jnp.exp(ls_ref[...])
    mag = jnp.exp(lr * step)
    a_re = mag * jnp.cos(li * step)
    a_im = mag * jnp.sin(li * step)
    den = lr * lr + li * li
    coef_re = ((a_re - 1.0) * lr + a_im * li) / den
    coef_im = (a_im * lr - (a_re - 1.0) * li) / den
    br, bi = br_ref[...], bi_ref[...]
    ar_ref[...] = a_re
    ai_ref[...] = a_im
    bbr_ref[...] = coef_re * br - coef_im * bi
    bbi_ref[...] = coef_re * bi + coef_im * br


def _s5_prep(lam_re, lam_im, log_step, b_re, b_im, c_re, c_im):
    g, p = lam_re.shape
    hh = b_re.shape[-1]
    a_re, a_im, bb_re, bb_im = pl.pallas_call(
        _s5_prep_kernel,
        out_shape=[jax.ShapeDtypeStruct((g, 1, p), F32)] * 2 + [jax.ShapeDtypeStruct((g, hh, p), F32)] * 2,
    )(lam_re.reshape(g, 1, p), lam_im.reshape(g, 1, p), log_step.reshape(g, 1, 1),
      b_re.transpose(0, 2, 1), b_im.transpose(0, 2, 1))
    a = jnp.concatenate([a_re.reshape(1, g * p), a_im.reshape(1, g * p)], axis=1)
    eye = jnp.eye(g, dtype=bool)

    def blockdiag_in(bb):
        return jnp.where(eye[:, None, :, None], bb[:, :, None, :], 0.0).reshape(g * hh, g * p)

    def blockdiag_out(c):
        return jnp.where(eye[:, None, :, None], c.transpose(0, 2, 1)[:, :, None, :], 0.0).reshape(g * p, g * hh)

    bmat = jnp.concatenate([blockdiag_in(bb_re), blockdiag_in(bb_im)], axis=1).astype(BF16)
    cmat = jnp.concatenate([blockdiag_out(c_re), -blockdiag_out(c_im)], axis=0).astype(BF16)
    return a, bmat, cmat


def _s5_tail(y, u, d, wg, bg):
    y = jax.nn.gelu(y + d * u)
    return y * jax.nn.sigmoid(_dot(y, wg) + bg)


def _s5_scan_kernel(u_ref, h0_ref, a_ref, bmat_ref, cmat_ref, d_ref, wg_ref, bg_ref,
                    y_ref, hout_ref, col_ref, up_ref, bu_ref, pw_ref, hin_ref, carry_ref):
    n_seg, seg = S5_SEGMENTS, S5_SEG_LEN
    ns = a_ref.shape[1] // 2
    lw = S5_LANE_CHUNK
    first_chunk = pl.program_id(1) == 0

    @pl.when(jnp.logical_and(pl.program_id(0) == 0, first_chunk))
    def _():
        ar, ai = a_ref[:, 0:ns], a_ref[:, ns:2 * ns]
        pw_ref[0:1, :] = a_ref[...]

        def body(k, c):
            pr, pi = c
            nr, ni = pr * ar - pi * ai, pr * ai + pi * ar
            pw_ref[pl.ds(k, 1), 0:ns] = nr
            pw_ref[pl.ds(k, 1), ns:2 * ns] = ni
            return nr, ni

        lax.fori_loop(1, seg, body, (ar, ai))

    @pl.when(first_chunk)
    def _():
        carry_ref[...] = h0_ref[...]

    n_col = u_ref.shape[1] // LANES
    for k in range(n_col):
        col_ref[k] = u_ref[:, k * LANES:(k + 1) * LANES]
    for t in range(seg):
        for k in range(n_col):
            up_ref[t * n_seg:(t + 1) * n_seg, k * LANES:(k + 1) * LANES] = col_ref[k, pl.ds(t, n_seg, stride=seg), :]
    bu_ref[...] = _dot(up_ref[...], bmat_ref[...])

    for lc in range(0, ns, lw):
        ar = jnp.broadcast_to(a_ref[:, lc:lc + lw], (n_seg, lw))
        ai = jnp.broadcast_to(a_ref[:, ns + lc:ns + lc + lw], (n_seg, lw))

        def sweep(t, c, lc=lc, ar=ar, ai=ai):
            hr, hi = c
            r0 = pl.multiple_of(t * n_seg, n_seg)
            nr = ar * hr - ai * hi + bu_ref[pl.ds(r0, n_seg), lc:lc + lw]
            ni = ar * hi + ai * hr + bu_ref[pl.ds(r0, n_seg), ns + lc:ns + lc + lw]
            bu_ref[pl.ds(r0, n_seg), lc:lc + lw] = nr
            bu_ref[pl.ds(r0, n_seg), ns + lc:ns + lc + lw] = ni
            return nr, ni

        zero = jnp.zeros((n_seg, lw), F32)
        lax.fori_loop(0, seg, sweep, (zero, zero), unroll=2)

    asr, asi = pw_ref[seg - 1:seg, 0:ns], pw_ref[seg - 1:seg, ns:2 * ns]
    cr, ci = carry_ref[:, 0:ns], carry_ref[:, ns:2 * ns]
    last = (seg - 1) * n_seg
    for s in range(n_seg):
        hin_ref[s:s + 1, 0:ns] = cr
        hin_ref[s:s + 1, ns:2 * ns] = ci
        er = bu_ref[last + s:last + s + 1, 0:ns]
        ei = bu_ref[last + s:last + s + 1, ns:2 * ns]
        cr, ci = asr * cr - asi * ci + er, asr * ci + asi * cr + ei
    carry_ref[:, 0:ns] = cr
    carry_ref[:, ns:2 * ns] = ci
    hout_ref[...] = carry_ref[...]

    for lc in range(0, ns, lw):
        hr_in = hin_ref[:, lc:lc + lw]
        hi_in = hin_ref[:, ns + lc:ns + lc + lw]

        def fix(t, c, lc=lc, hr_in=hr_in, hi_in=hi_in):
            r0 = pl.multiple_of(t * n_seg, n_seg)
            pr = jnp.broadcast_to(pw_ref[pl.ds(t, 1), lc:lc + lw], (n_seg, lw))
            pi = jnp.broadcast_to(pw_ref[pl.ds(t, 1), ns + lc:ns + lc + lw], (n_seg, lw))
            bu_ref[pl.ds(r0, n_seg), lc:lc + lw] += pr * hr_in - pi * hi_in
            bu_ref[pl.ds(r0, n_seg), ns + lc:ns + lc + lw] += pr * hi_in + pi * hr_in
            return c

        lax.fori_loop(0, seg, fix, 0, unroll=2)

    y = _dot(bu_ref[...], cmat_ref[...])
    out = _s5_tail(y, up_ref[...], d_ref[...], wg_ref[...], bg_ref[...])
    for k in range(n_col):
        col_ref[k] = out[:, k * LANES:(k + 1) * LANES]
    for r0 in range(0, n_seg * seg, n_seg):
        s, t0 = divmod(r0, seg)
        for k in range(n_col):
            y_ref[r0:r0 + n_seg, k * LANES:(k + 1) * LANES] = col_ref[k, pl.ds(t0 * n_seg + s, n_seg, stride=n_seg), :]


def _s5_scan(u, h0, a, bmat, cmat, d, wg, bg):
    bsz, seq, w = u.shape
    q = S5_SEGMENTS * S5_SEG_LEN
    ns2 = a.shape[1]
    return pl.pallas_call(
        _s5_scan_kernel,
        grid=(bsz, seq // q),
        in_specs=[pl.BlockSpec((None, q, w), lambda b, c: (b, c, 0)),
                  pl.BlockSpec((None, 1, ns2), lambda b, c: (b, 0, 0)),
                  _resident(a.shape), _resident(bmat.shape), _resident(cmat.shape),
                  _resident((1, w)), _resident(wg.shape), _resident((1, w))],
        out_specs=[pl.BlockSpec((None, q, w), lambda b, c: (b, c, 0)),
                   pl.BlockSpec((None, 1, ns2), lambda b, c: (b, 0, 0))],
        out_shape=[jax.ShapeDtypeStruct((bsz, seq, w), F32), jax.ShapeDtypeStruct((bsz, 1, ns2), F32)],
        scratch_shapes=[pltpu.VMEM((w // LANES, q, LANES), F32), pltpu.VMEM((q, w), F32), pltpu.VMEM((q, ns2), F32),
                        pltpu.VMEM((S5_SEG_LEN, ns2), F32), pltpu.VMEM((S5_SEGMENTS, ns2), F32),
                        pltpu.VMEM((1, ns2), F32)],
        compiler_params=_params("arbitrary", "arbitrary"),
    )(u, h0, a, bmat, cmat, d.reshape(1, w), wg, bg.reshape(1, w))


def _s5_step_kernel(u_ref, h0_ref, a_ref, bmat_ref, cmat_ref, d_ref, wg_ref, bg_ref, y_ref, hout_ref):
    ns = a_ref.shape[1] // 2
    u = u_ref[...]
    bu = _dot(u, bmat_ref[...])
    ar, ai = a_ref[:, 0:ns], a_ref[:, ns:2 * ns]
    h0r, h0i = h0_ref[:, 0:ns], h0_ref[:, ns:2 * ns]
    hr = ar * h0r - ai * h0i + bu[:, 0:ns]
    hi = ar * h0i + ai * h0r + bu[:, ns:2 * ns]
    hout_ref[:, 0:ns] = hr
    hout_ref[:, ns:2 * ns] = hi
    y = _dot(hout_ref[...], cmat_ref[...])
    y_ref[...] = _s5_tail(y, u, d_ref[...], wg_ref[...], bg_ref[...])


def _s5_step(u, h0, a, bmat, cmat, d, wg, bg):
    bsz, w = u.shape
    return pl.pallas_call(
        _s5_step_kernel,
        out_shape=[jax.ShapeDtypeStruct((bsz, w), F32), jax.ShapeDtypeStruct(h0.shape, F32)],
        compiler_params=pltpu.CompilerParams(vmem_limit_bytes=VMEM_LIMIT_BYTES),
    )(u, h0, a, bmat, cmat, d.reshape(1, w), wg, bg.reshape(1, w))


def _head_expander(n_heads):
    rows = lax.broadcasted_iota(jnp.int32, (LANES, n_heads * SSD_HEAD_DIM), 0)
    cols = lax.broadcasted_iota(jnp.int32, (LANES, n_heads * SSD_HEAD_DIM), 1)
    return (cols // SSD_HEAD_DIM == rows).astype(F32)


def _ssd_scan_kernel(z_ref, xbc_ref, dt_ref, cs_ref, h0_ref, cw_ref, cb_ref, dtb_ref, alog_ref, dx_ref, nw_ref,
                     y_ref, cso_ref, ho_ref, xpad_ref, st_ref, *, n_heads):
    q = z_ref.shape[0]
    width = n_heads * SSD_HEAD_DIM
    n_state = (xbc_ref.shape[1] - width) // (2 * SSD_GROUPS)
    kc = cw_ref.shape[0]
    hg = n_heads // SSD_GROUPS
    gw = hg * SSD_HEAD_DIM
    c = pl.program_id(1)
    tail0 = SUBLANES - (kc - 1)

    @pl.when(c == 0)
    def _():
        xpad_ref[tail0:SUBLANES, :] = cs_ref[...]
        st_ref[...] = h0_ref[...].T

    @pl.when(c != 0)
    def _():
        xpad_ref[tail0:SUBLANES, :] = xpad_ref[q + tail0:q + SUBLANES, :]

    xbc = xbc_ref[...]
    xpad_ref[SUBLANES:SUBLANES + q, :] = xbc
    cso_ref[...] = xbc[q - (kc - 1):q, :]

    conv = cb_ref[...] + xbc * cw_ref[kc - 1:kc, :]
    for k in range(kc - 1):
        conv = conv + xpad_ref[tail0 + k:tail0 + k + q, :] * cw_ref[k:k + 1, :]
    conv = jax.nn.silu(conv)
    xs = conv[:, 0:width]
    bm = [conv[:, width + g * n_state:width + (g + 1) * n_state] for g in range(SSD_GROUPS)]
    cm = [conv[:, width + (SSD_GROUPS + g) * n_state:width + (SSD_GROUPS + g + 1) * n_state]
          for g in range(SSD_GROUPS)]

    dt = jax.nn.softplus(dt_ref[...] + dtb_ref[...])
    av = dt * (-jnp.exp(alog_ref[...]))
    rows = lax.broadcasted_iota(jnp.int32, (q, q), 0)
    cols = lax.broadcasted_iota(jnp.int32, (q, q), 1)
    causal = rows >= cols
    a_cs = _dot_exact(causal.astype(F32), av)
    a_cs_t = a_cs.T
    expand = _head_expander(n_heads)
    a_cs_x = _dot_exact(a_cs, expand)
    tot_x = a_cs_x[q - 1:q, :]
    xdt = xs * _dot_exact(dt, expand)
    cb = [_dot_nt(cm[g], bm[g]) for g in range(SSD_GROUPS)]

    y_parts = []
    for h in range(n_heads):
        g = h // hg
        seg = jnp.where(causal, a_cs[:, h:h + 1] - a_cs_t[h:h + 1, :], -jnp.inf)
        gmat = cb[g] * jnp.exp(seg)
        y_parts.append(_dot(gmat, xdt[:, h * SSD_HEAD_DIM:(h + 1) * SSD_HEAD_DIM]))
    y = jnp.concatenate(y_parts, axis=1)

    st = st_ref[...]
    y_off = jnp.concatenate([_dot(cm[g], st[:, g * gw:(g + 1) * gw]) for g in range(SSD_GROUPS)], axis=1)
    y = y + y_off * jnp.exp(a_cs_x) + dx_ref[...] * xs
    xdec = xdt * jnp.exp(tot_x - a_cs_x)
    upd = jnp.concatenate([_dot(bm[g].T, xdec[:, g * gw:(g + 1) * gw]) for g in range(SSD_GROUPS)], axis=1)
    st_new = st * jnp.exp(tot_x) + upd
    st_ref[...] = st_new

    @pl.when(c == pl.num_programs(1) - 1)
    def _():
        ho_ref[...] = st_new.T

    z = z_ref[...]
    y_ref[...] = _rms_norm(y * jax.nn.silu(z), nw_ref[...]).astype(y_ref.dtype)


def _pad_lanes(v):
    return jnp.pad(v.reshape(1, -1), ((0, 0), (0, LANES - v.shape[-1])))


def _ssd_scan(z, xbc, dt, conv_state, h0, conv_w, conv_b, dt_bias, a_log, d_skip, norm_w):
    bsz, seq, width = z.shape
    cdim = xbc.shape[2]
    n_heads = dt_bias.shape[0]
    kc = conv_w.shape[0]
    n_state = h0.shape[2]
    q = SSD_CHUNK
    blk = lambda n: pl.BlockSpec((None, q, n), lambda b, c: (b, c, 0))
    per_b = lambda s: pl.BlockSpec((None,) + s, lambda b, c: (b, 0, 0))
    return pl.pallas_call(
        functools.partial(_ssd_scan_kernel, n_heads=n_heads),
        grid=(bsz, seq // q),
        in_specs=[blk(width), blk(cdim), blk(LANES), per_b((kc - 1, cdim)), per_b((width, n_state)),
                  _resident((kc, cdim)), _resident((1, cdim)), _resident((1, LANES)), _resident((1, LANES)),
                  _resident((1, width)), _resident((1, width))],
        out_specs=[blk(width), per_b((kc - 1, cdim)), per_b((width, n_state))],
        out_shape=[jax.ShapeDtypeStruct((bsz, seq, width), BF16),
                   jax.ShapeDtypeStruct((bsz, kc - 1, cdim), F32),
                   jax.ShapeDtypeStruct((bsz, width, n_state), F32)],
        scratch_shapes=[pltpu.VMEM((q + SUBLANES, cdim), F32), pltpu.VMEM((n_state, width), F32)],
        compiler_params=_params("arbitrary", "arbitrary"),
    )(z, xbc, dt, conv_state, h0, conv_w, conv_b.reshape(1, cdim), _pad_lanes(dt_bias), _pad_lanes(a_log),
      jnp.repeat(d_skip, SSD_HEAD_DIM).reshape(1, width), norm_w.reshape(1, width))


def _ssd_pre_kernel(xbc_ref, dt_ref, cs_ref, cw_ref, cb_ref, dtb_ref, alog_ref,
                    conv_ref, cso_ref, dtx_ref, dax_ref, *, n_heads):
    kc = cw_ref.shape[0]
    width = n_heads * SSD_HEAD_DIM
    xbc = xbc_ref[...]
    conv = cb_ref[...] + xbc * cw_ref[kc - 1:kc, :]
    for k in range(kc - 1):
        conv = conv + cs_ref[k] * cw_ref[k:k + 1, :]
    for k in range(kc - 2):
        cso_ref[k] = cs_ref[k + 1]
    cso_ref[kc - 2] = xbc
    conv = jax.nn.silu(conv)
    conv_ref[...] = conv
    dt = jax.nn.softplus(dt_ref[...] + dtb_ref[...])
    expand = _head_expander(n_heads)
    dtx_ref[...] = conv[:, 0:width] * _dot_exact(dt, expand)
    dax_ref[...] = jnp.exp(_dot_exact(dt * (-jnp.exp(alog_ref[...])), expand))


def _ssd_state_kernel(st_ref, dtxt_ref, daxt_ref, conv_ref, z_ref, dx_ref, nw_ref, y_ref, sto_ref, *, n_heads):
    b = pl.program_id(0)
    width, n_state = st_ref.shape
    gw = width // SSD_GROUPS
    nb = dtxt_ref.shape[1]
    onehot = (lax.broadcasted_iota(jnp.int32, (nb, n_state), 0) == b).astype(F32)
    dtx_col = _dot_exact(dtxt_ref[...], onehot)
    da_col = _dot_exact(daxt_ref[...], onehot)
    row = conv_ref[pl.ds(b, 1), :]
    ys = []
    for g in range(SSD_GROUPS):
        r = slice(g * gw, (g + 1) * gw)
        bm = row[:, width + g * n_state:width + (g + 1) * n_state]
        cm = row[:, width + (SSD_GROUPS + g) * n_state:width + (SSD_GROUPS + g + 1) * n_state]
        new = st_ref[r, :] * da_col[r, :] + dtx_col[r, :] * bm
        sto_ref[r, :] = new
        ys.append(_dot_nt(jnp.broadcast_to(cm, (SUBLANES, n_state)), new)[0:1, :])
    xs = row[:, 0:width]
    y = jnp.concatenate(ys, axis=1) + dx_ref[...] * xs
    y_ref[...] = _rms_norm(y * jax.nn.silu(z_ref[pl.ds(b, 1), :]), nw_ref[...])


def _ssd_step(z, xbc, dt, conv_state, h0, conv_w, conv_b, dt_bias, a_log, d_skip, norm_w):
    bsz, width = z.shape
    cdim = xbc.shape[1]
    n_heads = dt_bias.shape[0]
    kc = conv_w.shape[0]
    n_state = h0.shape[2]
    conv, new_conv, dtx, dax = pl.pallas_call(
        functools.partial(_ssd_pre_kernel, n_heads=n_heads),
        out_shape=[jax.ShapeDtypeStruct((bsz, cdim), F32), jax.ShapeDtypeStruct((kc - 1, bsz, cdim), F32),
                   jax.ShapeDtypeStruct((bsz, width), F32), jax.ShapeDtypeStruct((bsz, width), F32)],
        compiler_params=pltpu.CompilerParams(vmem_limit_bytes=VMEM_LIMIT_BYTES),
    )(xbc, dt, conv_state.transpose(1, 0, 2), conv_w, conv_b.reshape(1, cdim), _pad_lanes(dt_bias),
      _pad_lanes(a_log))
    pad_t = lambda m: jnp.pad(m, ((0, LANES - bsz), (0, 0))).T
    y, st = pl.pallas_call(
        functools.partial(_ssd_state_kernel, n_heads=n_heads),
        grid=(bsz,),
        in_specs=[pl.BlockSpec((None, width, n_state), lambda b: (b, 0, 0)),
                  _resident((width, LANES)), _resident((width, LANES)), _resident((bsz, cdim)),
                  _resident((bsz, width)), _resident((1, width)), _resident((1, width))],
        out_specs=[pl.BlockSpec((None, 1, width), lambda b: (b, 0, 0)),
                   pl.BlockSpec((None, width, n_state), lambda b: (b, 0, 0))],
        out_shape=[jax.ShapeDtypeStruct((bsz, 1, width), F32), jax.ShapeDtypeStruct(h0.shape, F32)],
        compiler_params=_params("arbitrary"),
    )(h0, pad_t(dtx), pad_t(dax), conv, z, jnp.repeat(d_skip, SSD_HEAD_DIM).reshape(1, width),
      norm_w.reshape(1, width))
    return y.reshape(bsz, width), new_conv.transpose(1, 0, 2), st


def _diff_lambda(lq1_ref, lk1_ref, lq2_ref, lk2_ref, lam_init):
    s1 = jnp.sum(lq1_ref[...] * lk1_ref[...], axis=-1, keepdims=True)
    s2 = jnp.sum(lq2_ref[...] * lk2_ref[...], axis=-1, keepdims=True)
    return jnp.exp(s1) - jnp.exp(s2) + lam_init


def _flash_kernel(q_ref, k_ref, v_ref, lq1_ref, lk1_ref, lq2_ref, lk2_ref, g_ref, o_ref, *, lam_init):
    tq, tk, dh = ATTN_TQ, ATTN_TK, DA_HEAD_DIM
    qi = pl.program_id(2)
    q = q_ref[...]
    qs = (q[:, 0:dh], q[:, dh:2 * dh])

    def block(j, carry, masked):
        k0 = pl.multiple_of(j * tk, tk)
        kb = k_ref[pl.ds(k0, tk), :]
        vb = v_ref[pl.ds(k0, tk), :]
        out = []
        for c in range(2):
            m, l, acc = carry[c]
            s = _dot_nt(qs[c], kb[:, c * dh:(c + 1) * dh])
            if masked:
                rows = lax.broadcasted_iota(jnp.int32, (tq, tk), 0)
                cols = lax.broadcasted_iota(jnp.int32, (tq, tk), 1)
                s = jnp.where(rows >= cols, s, NEG_BIG)
            m_new = jnp.maximum(m, jnp.max(s, axis=-1, keepdims=True))
            alpha = jnp.exp(m - m_new)
            p = jnp.exp(s - m_new)
            l = alpha * l + jnp.sum(p, axis=-1, keepdims=True)
            acc = alpha * acc + _dot(p, vb)
            out.append((m_new, l, acc))
        return tuple(out)

    init = tuple((jnp.full((tq, 1), NEG_BIG, F32), jnp.zeros((tq, 1), F32), jnp.zeros((tq, 2 * dh), F32))
                 for _ in range(2))
    carry = lax.fori_loop(0, qi * (tq // tk), functools.partial(block, masked=False), init)
    (_, l0, a0), (_, l1, a1) = block(qi * (tq // tk), carry, masked=True)
    lam = _diff_lambda(lq1_ref, lk1_ref, lq2_ref, lk2_ref, lam_init)
    o = a0 / l0 - lam * (a1 / l1)
    o_ref[...] = (_rms_norm(o, g_ref[...]) * (1.0 - lam_init)).astype(o_ref.dtype)


def _flash(qb, kb, vb, lq1, lk1, lq2, lk2, subln_w, lam_init):
    assert ATTN_TQ == ATTN_TK
    bsz, seq, width = qb.shape
    hw = 2 * DA_HEAD_DIM
    vec = lambda a: a.reshape(1, -1)
    kv = pl.BlockSpec((None, seq, hw), lambda b, h, i: (b, 0, h))
    qo = pl.BlockSpec((None, ATTN_TQ, hw), lambda b, h, i: (b, i, h))
    small = _resident((1, DA_HEAD_DIM))
    return pl.pallas_call(
        functools.partial(_flash_kernel, lam_init=lam_init),
        grid=(bsz, width // hw, seq // ATTN_TQ),
        in_specs=[qo, kv, kv, small, small, small, small, _resident((1, hw))],
        out_specs=qo,
        out_shape=jax.ShapeDtypeStruct((bsz, seq, width), BF16),
        compiler_params=_params("parallel", "parallel", "arbitrary"),
    )(qb, kb, vb, vec(lq1), vec(lk1), vec(lq2), vec(lk2), vec(subln_w))


def _decode_attn_kernel(pt_ref, q_ref, kn_ref, vn_ref, lq1_ref, lk1_ref, lq2_ref, lk2_ref, g_ref, *rest,
                        lam_init, n_pages):
    k_refs = rest[0:n_pages]
    v_refs = rest[n_pages:2 * n_pages]
    o_ref, m_ref, l_ref, acc_ref = rest[2 * n_pages:]
    b, s = pl.program_id(0), pl.program_id(1)
    width = q_ref.shape[1]
    n_sub = width // DA_HEAD_DIM
    hw = 2 * DA_HEAD_DIM
    rows = lax.broadcasted_iota(jnp.int32, (n_sub, width), 0)
    lanes = lax.broadcasted_iota(jnp.int32, (n_sub, width), 1)
    qrow = q_ref[pl.ds(b, 1), :]
    qm = jnp.where(lanes // DA_HEAD_DIM == rows, jnp.broadcast_to(qrow, (n_sub, width)), 0.0).astype(BF16)

    @pl.when(s == 0)
    def _():
        kn = jnp.broadcast_to(kn_ref[pl.ds(b, 1), :], (SUBLANES, width))
        m_ref[...] = _dot_nt(qm, kn)[:, 0:1]
        l_ref[...] = jnp.ones_like(l_ref)
        acc_ref[...] = jnp.broadcast_to(vn_ref[pl.ds(b, 1), :], (n_sub, width))

    scores = [_dot_nt(qm, k_ref[...]) for k_ref in k_refs]
    m_old = m_ref[...]
    m_new = m_old
    for sc in scores:
        m_new = jnp.maximum(m_new, jnp.max(sc, axis=-1, keepdims=True))
    alpha = jnp.exp(m_old - m_new)
    l = alpha * l_ref[...]
    acc = alpha * acc_ref[...]
    for sc, v_ref in zip(scores, v_refs):
        p = jnp.exp(sc - m_new)
        l = l + jnp.sum(p, axis=-1, keepdims=True)
        acc = acc + _dot(p, v_ref[...])
    m_ref[...] = m_new
    l_ref[...] = l
    acc_ref[...] = acc

    @pl.when(s == pl.num_programs(1) - 1)
    def _():
        lam = _diff_lambda(lq1_ref, lk1_ref, lq2_ref, lk2_ref, lam_init)
        o = acc / l
        own = lanes // hw == rows // 2
        coef = jnp.where(rows % 2 == 0, 1.0, -lam)
        out = jnp.sum(jnp.where(own, coef * o, 0.0), axis=0, keepdims=True)
        g = g_ref[...]
        parts = [_rms_norm(out[:, h * hw:(h + 1) * hw], g) for h in range(width // hw)]
        o_ref[...] = jnp.concatenate(parts, axis=1) * (1.0 - lam_init)


def _decode_attn(page_table, q, k_new, v_new, cache_k, cache_v, layer, lq1, lk1, lq2, lk2, subln_w, lam_init):
    bsz, width = q.shape
    n_tab = page_table.shape[1]
    page = cache_k.shape[2]
    npg = DEC_PAGES_PER_STEP
    hw = 2 * DA_HEAD_DIM
    vec = lambda a: a.reshape(1, -1)
    res = lambda shape: pl.BlockSpec(shape, lambda b, s, pt: (0,) * len(shape))

    def page_spec(i):
        return pl.BlockSpec((None, None, page, width),
                            lambda b, s, pt: (layer, pt[b * n_tab + s * npg + i], 0, 0))

    pages = [page_spec(i) for i in range(npg)]
    n_sub = width // DA_HEAD_DIM
    grid_spec = pltpu.PrefetchScalarGridSpec(
        num_scalar_prefetch=1,
        grid=(bsz, n_tab // npg),
        in_specs=[res((bsz, width))] * 3 + [res((1, DA_HEAD_DIM))] * 4 + [res((1, hw))] + pages + pages,
        out_specs=pl.BlockSpec((None, 1, width), lambda b, s, pt: (b, 0, 0)),
        scratch_shapes=[pltpu.VMEM((n_sub, 1), F32), pltpu.VMEM((n_sub, 1), F32), pltpu.VMEM((n_sub, width), F32)],
    )
    out = pl.pallas_call(
        functools.partial(_decode_attn_kernel, lam_init=lam_init, n_pages=npg),
        grid_spec=grid_spec,
        out_shape=jax.ShapeDtypeStruct((bsz, 1, width), F32),
        compiler_params=_params("arbitrary", "arbitrary"),
    )(page_table.reshape(-1), q, k_new, v_new, vec(lq1), vec(lk1), vec(lq2), vec(lk2), vec(subln_w),
      *([cache_k] * npg), *([cache_v] * npg))
    return out.reshape(bsz, width)


def kernel(x_prompt, x_sample, state_s5_re, state_s5_im, state_ssd, state_conv, cache_k, cache_v, page_table, p_prompt, p_sample, ln1_g, ln1_b, ln2_g, ln2_b, ssm_w_in, s5_lam_re, s5_lam_im, s5_log_step, s5_b_re, s5_b_im, s5_c_re, s5_c_im, s5_d, s5_w_glu, s5_b_glu, ssd_conv_w, ssd_conv_b, ssd_dt_bias, ssd_a_log, ssd_d, ssd_norm_w, ssm_w_out, attn_w_qkv, attn_lq1, attn_lk1, attn_lq2, attn_lk2, attn_subln_w, attn_w_out, mlp_w1, mlp_w2, ple_w_proj, ple_w_gate):
    bp, seq_p, d_model = x_prompt.shape
    bs, seq_s, _ = x_sample.shape
    assert seq_s == 1
    depth = ln1_g.shape[0]
    alpha = (2 * depth) ** 0.25
    s5_w = s5_d.shape[1]
    s5_groups, s5_state = s5_lam_re.shape[1:]
    ns = s5_groups * s5_state
    ssd_heads = ssd_dt_bias.shape[1]
    ssd_w = ssd_heads * SSD_HEAD_DIM
    cdim = ssd_conv_w.shape[2]
    kc = ssd_conv_w.shape[1]
    n_state = state_ssd.shape[-1]
    da_w = attn_w_out.shape[1]
    da_heads = da_w // (2 * DA_HEAD_DIM)
    page = cache_k.shape[2]
    past_len = page_table.shape[1] * page
    tp = bp * seq_p

    xp = x_prompt.reshape(tp, d_model)
    xs = x_sample.reshape(bs, d_model)
    tables_p = _rope_tables(jnp.arange(seq_p))
    tables_s = _rope_tables(jnp.full((bs,), past_len))
    ck = cache_k.reshape(cache_k.shape[0], cache_k.shape[1], page, da_w)
    cv = cache_v.reshape(cache_v.shape[0], cache_v.shape[1], page, da_w)

    outs = {name: [] for name in ("s5p", "s5s", "ssdp", "ssds", "convp", "convs", "kp", "vp", "ks", "vs")}
    for i in range(depth):
        j = i // 2
        if i % 2 == 0:
            w_in = jnp.pad(ssm_w_in[j], ((0, 0), (0, LANES - ssd_heads))).astype(BF16)
            splits = (s5_w, ssd_w, cdim, LANES)
            a, bmat, cmat = _s5_prep(s5_lam_re[j], s5_lam_im[j], s5_log_step[j], s5_b_re[j], s5_b_im[j],
                                     s5_c_re[j], s5_c_im[j])
            wg = s5_w_glu[j].astype(BF16)
            ssd_prm = (ssd_conv_w[j], ssd_conv_b[j], ssd_dt_bias[j], ssd_a_log[j], ssd_d[j], ssd_norm_w[j])

            u, z, xbc, dt = _proj(xp, w_in, splits)
            ya, hfin = _s5_scan(u.reshape(bp, seq_p, s5_w), jnp.zeros((bp, 1, 2 * ns), F32), a, bmat, cmat,
                                s5_d[j], wg, s5_b_glu[j])
            yb, conv_p, st_p = _ssd_scan(z.reshape(bp, seq_p, ssd_w), xbc.reshape(bp, seq_p, cdim),
                                         dt.reshape(bp, seq_p, LANES), jnp.zeros((bp, kc - 1, cdim), F32),
                                         jnp.zeros((bp, ssd_w, n_state), F32), *ssd_prm)
            mixes_p = [ya.reshape(tp, s5_w), yb.reshape(tp, ssd_w)]
            outs["s5p"].append(hfin.reshape(bp, 2, s5_groups, s5_state))
            outs["ssdp"].append(st_p.reshape(bp, ssd_heads, SSD_HEAD_DIM, n_state))
            outs["convp"].append(conv_p)

            u, z, xbc, dt = _proj(xs, w_in, splits)
            h0 = jnp.concatenate([state_s5_re[j].reshape(bs, ns), state_s5_im[j].reshape(bs, ns)], axis=1)
            ya, hnew = _s5_step(u, h0, a, bmat, cmat, s5_d[j], wg, s5_b_glu[j])
            yb, conv_s, st_s = _ssd_step(z, xbc, dt, state_conv[j], state_ssd[j].reshape(bs, ssd_w, n_state),
                                         *ssd_prm)
            mixes_s = [ya, yb]
            outs["s5s"].append(hnew.reshape(bs, 2, s5_groups, s5_state))
            outs["ssds"].append(st_s.reshape(bs, ssd_heads, SSD_HEAD_DIM, n_state))
            outs["convs"].append(conv_s)
            w_out = ssm_w_out[j].astype(BF16)
            w_outs = [w_out[0:s5_w], w_out[s5_w:]]
        else:
            lam_init = 0.8 - 0.6 * math.exp(-0.3 * i)
            w_qkv = attn_w_qkv[j].astype(BF16)
            lam_prm = (attn_lq1[j], attn_lk1[j], attn_lq2[j], attn_lk2[j], attn_subln_w[j])

            k, v, qb, kb, vb = _qkv(xp, w_qkv, tables_p, BF16)
            shape3 = (bp, seq_p, da_w)
            o = _flash(qb.reshape(shape3), kb.reshape(shape3), vb.reshape(shape3), *lam_prm, lam_init)
            mixes_p = [o.reshape(tp, da_w)]
            outs["kp"].append(k.reshape(bp, seq_p, da_heads, 2 * DA_HEAD_DIM))
            outs["vp"].append(v.reshape(bp, seq_p, da_heads, 2 * DA_HEAD_DIM))

            k, v, qf, _, _ = _qkv(xs, w_qkv, tables_s, F32)
            o = _decode_attn(page_table, qf, k, v, ck, cv, j, *lam_prm, lam_init)
            mixes_s = [o]
            outs["ks"].append(k.reshape(bs, 1, da_heads, 2 * DA_HEAD_DIM))
            outs["vs"].append(v.reshape(bs, 1, da_heads, 2 * DA_HEAD_DIM))
            w_outs = [attn_w_out[j].astype(BF16)]

        tail = (ln1_g[i], ln1_b[i], ln2_g[i], ln2_b[i], mlp_w1[i].astype(BF16), mlp_w2[i].astype(BF16),
                ple_w_proj[i].astype(BF16), ple_w_gate[i].astype(BF16), alpha)
        xp = _finish(xp, mixes_p, w_outs, p_prompt[i].reshape(tp, -1), *tail)
        xs = _finish(xs, mixes_s, w_outs, p_sample[i].reshape(bs, -1), *tail)

    s5p, s5s = jnp.stack(outs["s5p"]), jnp.stack(outs["s5s"])
    return (xp.reshape(bp, seq_p, d_model), xs.reshape(bs, 1, d_model),
            s5p[:, :, 0], s5p[:, :, 1], s5s[:, :, 0], s5s[:, :, 1],
            jnp.stack(outs["ssdp"]), jnp.stack(outs["ssds"]), jnp.stack(outs["convp"]), jnp.stack(outs["convs"]),
            jnp.stack(outs["kp"]), jnp.stack(outs["vp"]), jnp.stack(outs["ks"]), jnp.stack(outs["vs"]))
```

```python
import functools
import math

import jax
import jax.numpy as jnp
from jax import lax
from jax.experimental import pallas as pl
from jax.experimental.pallas import tpu as pltpu

F32 = jnp.float32
BF16 = jnp.bfloat16

S5_GROUP = 16
SSD_HEAD_DIM = 64
SSD_GROUPS = 2
DA_HEAD_DIM = 64
ROT_DIM = DA_HEAD_DIM // 4
ROPE_THETA = 500000.0
LN_EPS = 1e-5
RMS_EPS = 1e-5

LANES = 128
SUBLANES = 8
VMEM_LIMIT_BYTES = 56 * 1024 * 1024

ROW_TILE = 512
S5_SEGMENTS = SUBLANES
S5_SEG_LEN = 64
S5_LANE_CHUNK = 512
SSD_CHUNK = 256
ATTN_TQ = 512
ATTN_TK = 512
ATTN_Q_STRIPE = 256
DEC_PAGES_PER_STEP = 8
NEG_BIG = -1e30


def _resident(shape):
    zeros = (0,) * len(shape)
    return pl.BlockSpec(shape, lambda *_: zeros, pipeline_mode=pl.Buffered(1))


def _params(*sem):
    return pltpu.CompilerParams(dimension_semantics=sem, vmem_limit_bytes=VMEM_LIMIT_BYTES)


def _dot(a, b):
    return jnp.dot(a.astype(BF16), b.astype(BF16), preferred_element_type=F32)


def _dot_nt(a, b):
    return lax.dot_general(a.astype(BF16), b.astype(BF16), (((1,), (1,)), ((), ())),
                           preferred_element_type=F32)


def _split3(x):
    hi = x.astype(BF16)
    r = x - hi.astype(F32)
    mid = r.astype(BF16)
    lo = (r - mid.astype(F32)).astype(BF16)
    return hi, mid, lo


def _dot_exact(a, b, selector):
    sel = (a, b)[selector].astype(BF16)
    pieces = _split3((b, a)[selector])
    dot = (lambda p: jnp.dot(sel, p, preferred_element_type=F32)) if selector == 0 else \
          (lambda p: jnp.dot(p, sel, preferred_element_type=F32))
    return dot(pieces[0]) + dot(pieces[1]) + dot(pieces[2])


def _sigmoid(x):
    return 0.5 * jnp.tanh(0.5 * x) + 0.5


def _silu(x):
    return x * _sigmoid(x)


def _layer_norm(x, g, b):
    mu = jnp.mean(x, -1, keepdims=True)
    xc = x - mu
    var = jnp.mean(xc * xc, -1, keepdims=True)
    return xc * lax.rsqrt(var + LN_EPS) * g + b


def _rms_norm(x, g):
    return x * lax.rsqrt(jnp.mean(x * x, -1, keepdims=True) + RMS_EPS) * g


def _row_tile(m):
    return ROW_TILE if m % ROW_TILE == 0 else m


def _proj_kernel(x_ref, w_ref, *o_refs, splits):
    xb = x_ref[...].astype(BF16)
    off = 0
    for o_ref, n in zip(o_refs, splits):
        o_ref[...] = jnp.dot(xb, w_ref[:, off:off + n], preferred_element_type=F32).astype(o_ref.dtype)
        off += n


def _proj(x, w, splits):
    m, k = x.shape
    tm = _row_tile(m)
    return pl.pallas_call(
        functools.partial(_proj_kernel, splits=splits),
        grid=(m // tm,),
        in_specs=[pl.BlockSpec((tm, k), lambda i: (i, 0)), _resident(w.shape)],
        out_specs=[pl.BlockSpec((tm, n), lambda i: (i, 0)) for n in splits],
        out_shape=[jax.ShapeDtypeStruct((m, n), F32) for n in splits],
        compiler_params=_params("parallel"),
    )(x, w)


def _rope(t, ca, sa, sb):
    half = ROT_DIM // 2
    outs = []
    for h in range(t.shape[1] // LANES):
        c = t[:, h * LANES:(h + 1) * LANES]
        up = pltpu.roll(c, LANES - half, 1)
        dn = pltpu.roll(c, half, 1)
        outs.append(c * ca + up * sa + dn * sb)
    return jnp.concatenate(outs, axis=1)


def _qkv_rows(x_ref, w_ref, ca_ref, sa_ref, sb_ref, k_ref, v_ref, *, width, scale):
    xb = x_ref[...].astype(BF16)
    ca, sa, sb = ca_ref[...], sa_ref[...], sb_ref[...]
    q = _rope(jnp.dot(xb, w_ref[:, 0:width], preferred_element_type=F32), ca, sa, sb) * scale
    k = _rope(jnp.dot(xb, w_ref[:, width:2 * width], preferred_element_type=F32), ca, sa, sb)
    v = jnp.dot(xb, w_ref[:, 2 * width:3 * width], preferred_element_type=F32)
    tm = xb.shape[0]
    n_heads = width // LANES
    for h in range(n_heads):
        k_ref[pl.ds(h, tm, stride=n_heads), :] = k[:, h * LANES:(h + 1) * LANES]
        v_ref[pl.ds(h, tm, stride=n_heads), :] = v[:, h * LANES:(h + 1) * LANES]
    return q, k, v


def _qkv_prompt_kernel(x_ref, w_ref, ca_ref, sa_ref, sb_ref, k_ref, v_ref, qb_ref, kb_ref, vt_ref, **kw):
    q, k, v = _qkv_rows(x_ref, w_ref, ca_ref, sa_ref, sb_ref, k_ref, v_ref, **kw)
    qb_ref[...] = q.astype(BF16)
    kb_ref[...] = k.astype(BF16)
    vt_ref[...] = v.T.astype(BF16)


def _qkv_decode_kernel(x_ref, w_ref, ca_ref, sa_ref, sb_ref, k_ref, v_ref, q_ref, **kw):
    q, _, _ = _qkv_rows(x_ref, w_ref, ca_ref, sa_ref, sb_ref, k_ref, v_ref, **kw)
    q_ref[...] = q


def _rope_tables(pos):
    half = ROT_DIM // 2
    inv = ROPE_THETA ** (-jnp.arange(0, ROT_DIM, 2, dtype=F32) / ROT_DIM)
    ang = pos.astype(F32)[:, None] * inv
    cos, sin = jnp.cos(ang), jnp.sin(ang)
    n = pos.shape[0]
    pad = jnp.zeros((n, DA_HEAD_DIM - ROT_DIM), F32)
    zero = jnp.zeros((n, half), F32)
    ca = jnp.concatenate([cos, cos, pad + 1.0], axis=1)
    sa = jnp.concatenate([-sin, zero, pad], axis=1)
    sb = jnp.concatenate([zero, sin, pad], axis=1)
    rep = LANES // DA_HEAD_DIM
    return jnp.tile(ca, (1, rep)), jnp.tile(sa, (1, rep)), jnp.tile(sb, (1, rep))


def _qkv(x, w, tables, for_flash):
    m, d = x.shape
    width = w.shape[1] // 3
    n_heads = width // LANES
    tm = _row_tile(m)
    nt = tables[0].shape[0] // tm
    row = lambda i: (i, 0)
    tab = pl.BlockSpec((tm, LANES), lambda i: (i % nt, 0))
    full = pl.BlockSpec((tm, width), row)
    native = pl.BlockSpec((tm * n_heads, LANES), row)
    native_shape = jax.ShapeDtypeStruct((m * n_heads, LANES), F32)
    if for_flash:
        body = _qkv_prompt_kernel
        out_specs = [native, native, full, full, pl.BlockSpec((None, width, tm), lambda i: (i, 0, 0))]
        out_shape = [native_shape, native_shape, jax.ShapeDtypeStruct((m, width), BF16),
                     jax.ShapeDtypeStruct((m, width), BF16), jax.ShapeDtypeStruct((m // tm, width, tm), BF16)]
    else:
        body = _qkv_decode_kernel
        out_specs = [native, native, full]
        out_shape = [native_shape, native_shape, jax.ShapeDtypeStruct((m, width), F32)]
    return pl.pallas_call(
        functools.partial(body, width=width, scale=DA_HEAD_DIM ** -0.5 * math.log2(math.e)),
        grid=(m // tm,),
        in_specs=[pl.BlockSpec((tm, d), row), _resident(w.shape), tab, tab, tab],
        out_specs=out_specs,
        out_shape=out_shape,
        compiler_params=_params("parallel"),
    )(x, w, *tables)


def _finish_kernel(*refs, n_mix, alpha, ff_chunk):
    x_ref = refs[0]
    y_refs = refs[1:1 + n_mix]
    wo_refs = refs[1 + n_mix:1 + 2 * n_mix]
    (p_ref, g1_ref, b1_ref, g2_ref, b2_ref, w1_ref, w2_ref, wp_ref, wg_ref, o_ref) = refs[1 + 2 * n_mix:]
    mix = _dot(y_refs[0][...], wo_refs[0][...])
    for y_ref, wo_ref in zip(y_refs[1:], wo_refs[1:]):
        mix = mix + _dot(y_ref[...], wo_ref[...])
    x1 = _layer_norm(alpha * x_ref[...] + mix, g1_ref[...], b1_ref[...])
    x1b = x1.astype(BF16)
    h = None
    for c in range(0, w1_ref.shape[1], ff_chunk):
        a = jnp.maximum(jnp.dot(x1b, w1_ref[:, c:c + ff_chunk], preferred_element_type=F32), 0.0)
        part = jnp.dot((a * a).astype(BF16), w2_ref[c:c + ff_chunk, :], preferred_element_type=F32)
        h = part if h is None else h + part
    x2 = _layer_norm(alpha * x1 + h, g2_ref[...], b2_ref[...])
    gate = _sigmoid(_dot(x2, wg_ref[...]))
    o_ref[...] = x2 + gate * _dot(p_ref[...], wp_ref[...])


def _finish(x, mixes, w_outs, p, g1, b1, g2, b2, w1, w2, wp, wg, alpha):
    m, d = x.shape
    tm = _row_tile(m)
    row = lambda i: (i, 0)
    vec = lambda a: a.reshape(1, d)
    in_specs = ([pl.BlockSpec((tm, d), row)]
                + [pl.BlockSpec((tm, y.shape[1]), row) for y in mixes]
                + [_resident(w.shape) for w in w_outs]
                + [pl.BlockSpec((tm, p.shape[1]), row)]
                + [_resident((1, d))] * 4
                + [_resident(w.shape) for w in (w1, w2, wp, wg)])
    return pl.pallas_call(
        functools.partial(_finish_kernel, n_mix=len(mixes), alpha=alpha, ff_chunk=1024),
        grid=(m // tm,),
        in_specs=in_specs,
        out_specs=pl.BlockSpec((tm, d), row),
        out_shape=jax.ShapeDtypeStruct((m, d), F32),
        compiler_params=_params("parallel"),
    )(x, *mixes, *w_outs, p, vec(g1), vec(b1), vec(g2), vec(b2), w1, w2, wp, wg)


def _s5_prep_kernel(lr_ref, li_ref, ls_ref, br_ref, bi_ref, ar_ref, ai_ref, bbr_ref, bbi_ref):
    lr, li = lr_ref[...], li_ref[...]
    step = jnp.exp(ls_ref[...])
    mag = jnp.exp(lr * step)
    a_re = mag * jnp.cos(li * step)
    a_im = mag * jnp.sin(li * step)
    den = lr * lr + li * li
    coef_re = ((a_re - 1.0) * lr + a_im * li) / den
    coef_im = (a_im * lr - (a_re - 1.0) * li) / den
    br, bi = br_ref[...], bi_ref[...]
    ar_ref[...] = a_re
    ai_ref[...] = a_im
    bbr_ref[...] = coef_re * br - coef_im * bi
    bbi_ref[...] = coef_re * bi + coef_im * br


def _s5_prep(lam_re, lam_im, log_step, b_re, b_im, c_re, c_im):
    g, p = lam_re.shape
    hh = b_re.shape[-1]
    a_re, a_im, bb_re, bb_im = pl.pallas_call(
        _s5_prep_kernel,
        out_shape=[jax.ShapeDtypeStruct((g, 1, p), F32)] * 2 + [jax.ShapeDtypeStruct((g, hh, p), F32)] * 2,
    )(lam_re.reshape(g, 1, p), lam_im.reshape(g, 1, p), log_step.reshape(g, 1, 1),
      b_re.transpose(0, 2, 1), b_im.transpose(0, 2, 1))
    a = jnp.concatenate([a_re.reshape(1, g * p), a_im.reshape(1, g * p)], axis=1)
    gt = LANES // hh
    nt = g // gt
    eye = jnp.eye(gt, dtype=bool)

    def blockdiag_in(bb):
        bb = bb.reshape(nt, gt, hh, p)
        return jnp.where(eye[None, :, None, :, None], bb[:, :, :, None, :], 0.0).reshape(nt, gt * hh, gt * p)

    def blockdiag_out(c):
        c = c.transpose(0, 2, 1).reshape(nt, gt, p, hh)
        return jnp.where(eye[None, :, None, :, None], c[:, :, :, None, :], 0.0).reshape(nt, gt * p, gt * hh)

    bmat = jnp.concatenate([blockdiag_in(bb_re), blockdiag_in(bb_im)], axis=2).astype(BF16)
    cmat = jnp.concatenate([blockdiag_out(c_re), -blockdiag_out(c_im)], axis=1).astype(BF16)
    return a, bmat, cmat


def _s5_in_map(u, bmat_ref, store):
    for t in range(bmat_ref.shape[0]):
        part = _dot(u[:, t * LANES:(t + 1) * LANES], bmat_ref[t])
        s = part.shape[1] // 2
        store(t * s, part[:, 0:s], part[:, s:2 * s])


def _s5_out_map(load, cmat_ref):
    ys = []
    for t in range(cmat_ref.shape[0]):
        s = cmat_ref.shape[1] // 2
        hr, hi = load(t * s, s)
        ys.append(_dot(hr, cmat_ref[t, 0:s, :]) + _dot(hi, cmat_ref[t, s:2 * s, :]))
    return jnp.concatenate(ys, axis=1)


def _s5_tail(y, u, d, wg, bg):
    y = jax.nn.gelu(y + d * u)
    return y * _sigmoid(_dot(y, wg) + bg)


def _s5_scan_kernel(u_ref, h0_ref, a_ref, bmat_ref, cmat_ref, d_ref, wg_ref, bg_ref,
                    y_ref, hout_ref, col_ref, up_ref, bu_ref, pw_ref, hin_ref, carry_ref):
    n_seg, seg = S5_SEGMENTS, S5_SEG_LEN
    ns = a_ref.shape[1] // 2
    lw = S5_LANE_CHUNK
    first_chunk = pl.program_id(1) == 0

    @pl.when(jnp.logical_and(pl.program_id(0) == 0, first_chunk))
    def _():
        ar, ai = a_ref[:, 0:ns], a_ref[:, ns:2 * ns]
        pw_ref[0:1, :] = a_ref[...]

        def body(k, c):
            pr, pi = c
            nr, ni = pr * ar - pi * ai, pr * ai + pi * ar
            pw_ref[pl.ds(k, 1), 0:ns] = nr
            pw_ref[pl.ds(k, 1), ns:2 * ns] = ni
            return nr, ni

        lax.fori_loop(1, seg, body, (ar, ai))

    @pl.when(first_chunk)
    def _():
        carry_ref[...] = h0_ref[...]

    n_col = u_ref.shape[1] // LANES
    for k in range(n_col):
        col_ref[k] = u_ref[:, k * LANES:(k + 1) * LANES]
    for t in range(seg):
        for k in range(n_col):
            up_ref[t * n_seg:(t + 1) * n_seg, k * LANES:(k + 1) * LANES] = col_ref[k, pl.ds(t, n_seg, stride=seg), :]
    def store_bu(off, re, im):
        bu_ref[:, off:off + re.shape[1]] = re
        bu_ref[:, ns + off:ns + off + im.shape[1]] = im

    _s5_in_map(up_ref[...], bmat_ref, store_bu)

    for lc in range(0, ns, lw):
        ar = jnp.broadcast_to(a_ref[:, lc:lc + lw], (n_seg, lw))
        ai = jnp.broadcast_to(a_ref[:, ns + lc:ns + lc + lw], (n_seg, lw))

        def sweep(t, c, lc=lc, ar=ar, ai=ai):
            hr, hi = c
            r0 = pl.multiple_of(t * n_seg, n_seg)
            nr = ar * hr - ai * hi + bu_ref[pl.ds(r0, n_seg), lc:lc + lw]
            ni = ar * hi + ai * hr + bu_ref[pl.ds(r0, n_seg), ns + lc:ns + lc + lw]
            bu_ref[pl.ds(r0, n_seg), lc:lc + lw] = nr
            bu_ref[pl.ds(r0, n_seg), ns + lc:ns + lc + lw] = ni
            return nr, ni

        zero = jnp.zeros((n_seg, lw), F32)
        lax.fori_loop(0, seg, sweep, (zero, zero), unroll=2)

    asr, asi = pw_ref[seg - 1:seg, 0:ns], pw_ref[seg - 1:seg, ns:2 * ns]
    cr, ci = carry_ref[:, 0:ns], carry_ref[:, ns:2 * ns]
    last = (seg - 1) * n_seg
    for s in range(n_seg):
        hin_ref[s:s + 1, 0:ns] = cr
        hin_ref[s:s + 1, ns:2 * ns] = ci
        er = bu_ref[last + s:last + s + 1, 0:ns]
        ei = bu_ref[last + s:last + s + 1, ns:2 * ns]
        cr, ci = asr * cr - asi * ci + er, asr * ci + asi * cr + ei
    carry_ref[:, 0:ns] = cr
    carry_ref[:, ns:2 * ns] = ci
    hout_ref[...] = carry_ref[...]

    for lc in range(0, ns, lw):
        hr_in = hin_ref[:, lc:lc + lw]
        hi_in = hin_ref[:, ns + lc:ns + lc + lw]

        def fix(t, c, lc=lc, hr_in=hr_in, hi_in=hi_in):
            r0 = pl.multiple_of(t * n_seg, n_seg)
            pr = jnp.broadcast_to(pw_ref[pl.ds(t, 1), lc:lc + lw], (n_seg, lw))
            pi = jnp.broadcast_to(pw_ref[pl.ds(t, 1), ns + lc:ns + lc + lw], (n_seg, lw))
            bu_ref[pl.ds(r0, n_seg), lc:lc + lw] += pr * hr_in - pi * hi_in
            bu_ref[pl.ds(r0, n_seg), ns + lc:ns + lc + lw] += pr * hi_in + pi * hr_in
            return c

        lax.fori_loop(0, seg, fix, 0, unroll=2)

    y = _s5_out_map(lambda off, s: (bu_ref[:, off:off + s], bu_ref[:, ns + off:ns + off + s]), cmat_ref)
    out = _s5_tail(y, up_ref[...], d_ref[...], wg_ref[...], bg_ref[...])
    for k in range(n_col):
        col_ref[k] = out[:, k * LANES:(k + 1) * LANES]
    for r0 in range(0, n_seg * seg, n_seg):
        s, t0 = divmod(r0, seg)
        for k in range(n_col):
            y_ref[r0:r0 + n_seg, k * LANES:(k + 1) * LANES] = col_ref[k, pl.ds(t0 * n_seg + s, n_seg, stride=n_seg), :]


def _s5_scan(u, h0, a, bmat, cmat, d, wg, bg):
    bsz, seq, w = u.shape
    q = S5_SEGMENTS * S5_SEG_LEN
    ns2 = a.shape[1]
    return pl.pallas_call(
        _s5_scan_kernel,
        grid=(bsz, seq // q),
        in_specs=[pl.BlockSpec((None, q, w), lambda b, c: (b, c, 0)),
                  pl.BlockSpec((None, 1, ns2), lambda b, c: (b, 0, 0)),
                  _resident(a.shape), _resident(bmat.shape), _resident(cmat.shape),
                  _resident((1, w)), _resident(wg.shape), _resident((1, w))],
        out_specs=[pl.BlockSpec((None, q, w), lambda b, c: (b, c, 0)),
                   pl.BlockSpec((None, 1, ns2), lambda b, c: (b, 0, 0))],
        out_shape=[jax.ShapeDtypeStruct((bsz, seq, w), F32), jax.ShapeDtypeStruct((bsz, 1, ns2), F32)],
        scratch_shapes=[pltpu.VMEM((w // LANES, q, LANES), F32), pltpu.VMEM((q, w), F32), pltpu.VMEM((q, ns2), F32),
                        pltpu.VMEM((S5_SEG_LEN, ns2), F32), pltpu.VMEM((S5_SEGMENTS, ns2), F32),
                        pltpu.VMEM((1, ns2), F32)],
        compiler_params=_params("arbitrary", "arbitrary"),
    )(u, h0, a, bmat, cmat, d.reshape(1, w), wg, bg.reshape(1, w))


def _s5_step_kernel(u_ref, h0_ref, a_ref, bmat_ref, cmat_ref, d_ref, wg_ref, bg_ref, y_ref, hout_ref):
    ns = a_ref.shape[1] // 2
    u = u_ref[...]

    def advance(off, bu_re, bu_im):
        s = bu_re.shape[1]
        ar, ai = a_ref[:, off:off + s], a_ref[:, ns + off:ns + off + s]
        h0r, h0i = h0_ref[:, off:off + s], h0_ref[:, ns + off:ns + off + s]
        hout_ref[:, off:off + s] = ar * h0r - ai * h0i + bu_re
        hout_ref[:, ns + off:ns + off + s] = ar * h0i + ai * h0r + bu_im

    _s5_in_map(u, bmat_ref, advance)
    y = _s5_out_map(lambda off, s: (hout_ref[:, off:off + s], hout_ref[:, ns + off:ns + off + s]), cmat_ref)
    y_ref[...] = _s5_tail(y, u, d_ref[...], wg_ref[...], bg_ref[...])


def _s5_step(u, h0, a, bmat, cmat, d, wg, bg):
    bsz, w = u.shape
    return pl.pallas_call(
        _s5_step_kernel,
        out_shape=[jax.ShapeDtypeStruct((bsz, w), F32), jax.ShapeDtypeStruct(h0.shape, F32)],
        compiler_params=pltpu.CompilerParams(vmem_limit_bytes=VMEM_LIMIT_BYTES),
    )(u, h0, a, bmat, cmat, d.reshape(1, w), wg, bg.reshape(1, w))


def _head_expander(n_heads):
    rows = lax.broadcasted_iota(jnp.int32, (LANES, n_heads * SSD_HEAD_DIM), 0)
    cols = lax.broadcasted_iota(jnp.int32, (LANES, n_heads * SSD_HEAD_DIM), 1)
    return (cols // SSD_HEAD_DIM == rows).astype(F32)


def _ssd_scan_kernel(z_ref, xbc_ref, dt_ref, cs_ref, h0_ref, cw_ref, cb_ref, dtb_ref, alog_ref, dx_ref, nw_ref,
                     y_ref, cso_ref, ho_ref, xpad_ref, st_ref, *, n_heads):
    q = z_ref.shape[0]
    width = n_heads * SSD_HEAD_DIM
    n_state = (xbc_ref.shape[1] - width) // (2 * SSD_GROUPS)
    kc = cw_ref.shape[0]
    hg = n_heads // SSD_GROUPS
    gw = hg * SSD_HEAD_DIM
    c = pl.program_id(1)
    tail0 = SUBLANES - (kc - 1)

    @pl.when(c == 0)
    def _():
        xpad_ref[tail0:SUBLANES, :] = cs_ref[...]
        st_ref[...] = h0_ref[...].T

    @pl.when(c != 0)
    def _():
        xpad_ref[tail0:SUBLANES, :] = xpad_ref[q + tail0:q + SUBLANES, :]

    xbc = xbc_ref[...]
    xpad_ref[SUBLANES:SUBLANES + q, :] = xbc
    cso_ref[...] = xbc[q - (kc - 1):q, :]

    conv = cb_ref[...] + xbc * cw_ref[kc - 1:kc, :]
    for k in range(kc - 1):
        conv = conv + xpad_ref[tail0 + k:tail0 + k + q, :] * cw_ref[k:k + 1, :]
    conv = _silu(conv)
    xs = conv[:, 0:width]
    bm = [conv[:, width + g * n_state:width + (g + 1) * n_state] for g in range(SSD_GROUPS)]
    cm = [conv[:, width + (SSD_GROUPS + g) * n_state:width + (SSD_GROUPS + g + 1) * n_state]
          for g in range(SSD_GROUPS)]

    dt = jax.nn.softplus(dt_ref[...] + dtb_ref[...])
    av = dt * (-jnp.exp(alog_ref[...]))
    rows = lax.broadcasted_iota(jnp.int32, (q, q), 0)
    cols = lax.broadcasted_iota(jnp.int32, (q, q), 1)
    causal = rows >= cols
    a_cs = _dot_exact(causal.astype(F32), av, 0)
    a_cs_t = a_cs.T
    expand = _head_expander(n_heads)
    a_cs_x = _dot_exact(a_cs, expand, 1)
    tot_x = a_cs_x[q - 1:q, :]
    xdt = xs * _dot_exact(dt, expand, 1)
    cb = [_dot_nt(cm[g], bm[g]) for g in range(SSD_GROUPS)]

    y_parts = []
    for h in range(n_heads):
        g = h // hg
        seg = jnp.where(causal, a_cs[:, h:h + 1] - a_cs_t[h:h + 1, :], -jnp.inf)
        gmat = cb[g] * jnp.exp(seg)
        y_parts.append(_dot(gmat, xdt[:, h * SSD_HEAD_DIM:(h + 1) * SSD_HEAD_DIM]))
    y = jnp.concatenate(y_parts, axis=1)

    st = st_ref[...]
    y_off = jnp.concatenate([_dot(cm[g], st[:, g * gw:(g + 1) * gw]) for g in range(SSD_GROUPS)], axis=1)
    y = y + y_off * jnp.exp(a_cs_x) + dx_ref[...] * xs
    xdec = xdt * jnp.exp(tot_x - a_cs_x)
    upd = jnp.concatenate([_dot(bm[g].T, xdec[:, g * gw:(g + 1) * gw]) for g in range(SSD_GROUPS)], axis=1)
    st_new = st * jnp.exp(tot_x) + upd
    st_ref[...] = st_new

    @pl.when(c == pl.num_programs(1) - 1)
    def _():
        ho_ref[...] = st_new.T

    z = z_ref[...]
    y_ref[...] = _rms_norm(y * _silu(z), nw_ref[...]).astype(y_ref.dtype)


def _pad_lanes(v):
    return jnp.pad(v.reshape(1, -1), ((0, 0), (0, LANES - v.shape[-1])))


def _ssd_scan(z, xbc, dt, conv_state, h0, conv_w, conv_b, dt_bias, a_log, d_skip, norm_w):
    bsz, seq, width = z.shape
    cdim = xbc.shape[2]
    n_heads = dt_bias.shape[0]
    kc = conv_w.shape[0]
    n_state = h0.shape[2]
    q = SSD_CHUNK
    blk = lambda n: pl.BlockSpec((None, q, n), lambda b, c: (b, c, 0))
    per_b = lambda s: pl.BlockSpec((None,) + s, lambda b, c: (b, 0, 0))
    return pl.pallas_call(
        functools.partial(_ssd_scan_kernel, n_heads=n_heads),
        grid=(bsz, seq // q),
        in_specs=[blk(width), blk(cdim), blk(LANES), per_b((kc - 1, cdim)), per_b((width, n_state)),
                  _resident((kc, cdim)), _resident((1, cdim)), _resident((1, LANES)), _resident((1, LANES)),
                  _resident((1, width)), _resident((1, width))],
        out_specs=[blk(width), per_b((kc - 1, cdim)), per_b((width, n_state))],
        out_shape=[jax.ShapeDtypeStruct((bsz, seq, width), BF16),
                   jax.ShapeDtypeStruct((bsz, kc - 1, cdim), F32),
                   jax.ShapeDtypeStruct((bsz, width, n_state), F32)],
        scratch_shapes=[pltpu.VMEM((q + SUBLANES, cdim), F32), pltpu.VMEM((n_state, width), F32)],
        compiler_params=_params("arbitrary", "arbitrary"),
    )(z, xbc, dt, conv_state, h0, conv_w, conv_b.reshape(1, cdim), _pad_lanes(dt_bias), _pad_lanes(a_log),
      jnp.repeat(d_skip, SSD_HEAD_DIM).reshape(1, width), norm_w.reshape(1, width))


def _ssd_pre_kernel(xbc_ref, dt_ref, cs_ref, cw_ref, cb_ref, dtb_ref, alog_ref,
                    conv_ref, cso_ref, dtx_ref, dax_ref, *, n_heads):
    kc = cw_ref.shape[0]
    width = n_heads * SSD_HEAD_DIM
    xbc = xbc_ref[...]
    conv = cb_ref[...] + xbc * cw_ref[kc - 1:kc, :]
    for k in range(kc - 1):
        conv = conv + cs_ref[k] * cw_ref[k:k + 1, :]
    for k in range(kc - 2):
        cso_ref[k] = cs_ref[k + 1]
    cso_ref[kc - 2] = xbc
    conv = _silu(conv)
    conv_ref[...] = conv
    dt = jax.nn.softplus(dt_ref[...] + dtb_ref[...])
    expand = _head_expander(n_heads)
    dtx_ref[...] = conv[:, 0:width] * _dot_exact(dt, expand, 1)
    dax_ref[...] = jnp.exp(_dot_exact(dt * (-jnp.exp(alog_ref[...])), expand, 1))


def _ssd_state_kernel(st_ref, dtxt_ref, daxt_ref, conv_ref, z_ref, dx_ref, nw_ref, y_ref, sto_ref, *, n_heads):
    b = pl.program_id(0)
    width, n_state = st_ref.shape
    gw = width // SSD_GROUPS
    nb = dtxt_ref.shape[1]
    onehot = (lax.broadcasted_iota(jnp.int32, (nb, n_state), 0) == b).astype(F32)
    dtx_col = _dot_exact(dtxt_ref[...], onehot, 1)
    da_col = _dot_exact(daxt_ref[...], onehot, 1)
    row = conv_ref[pl.ds(b, 1), :]
    ys = []
    for g in range(SSD_GROUPS):
        r = slice(g * gw, (g + 1) * gw)
        bm = row[:, width + g * n_state:width + (g + 1) * n_state]
        cm = row[:, width + (SSD_GROUPS + g) * n_state:width + (SSD_GROUPS + g + 1) * n_state]
        new = st_ref[r, :] * da_col[r, :] + dtx_col[r, :] * bm
        sto_ref[r, :] = new
        ys.append(_dot_nt(jnp.broadcast_to(cm, (SUBLANES, n_state)), new)[0:1, :])
    xs = row[:, 0:width]
    y = jnp.concatenate(ys, axis=1) + dx_ref[...] * xs
    y_ref[...] = _rms_norm(y * _silu(z_ref[pl.ds(b, 1), :]), nw_ref[...])


def _ssd_step(z, xbc, dt, conv_state, h0, conv_w, conv_b, dt_bias, a_log, d_skip, norm_w):
    bsz, width = z.shape
    cdim = xbc.shape[1]
    n_heads = dt_bias.shape[0]
    kc = conv_w.shape[0]
    n_state = h0.shape[2]
    conv, new_conv, dtx, dax = pl.pallas_call(
        functools.partial(_ssd_pre_kernel, n_heads=n_heads),
        out_shape=[jax.ShapeDtypeStruct((bsz, cdim), F32), jax.ShapeDtypeStruct((kc - 1, bsz, cdim), F32),
                   jax.ShapeDtypeStruct((bsz, width), F32), jax.ShapeDtypeStruct((bsz, width), F32)],
        compiler_params=pltpu.CompilerParams(vmem_limit_bytes=VMEM_LIMIT_BYTES),
    )(xbc, dt, conv_state.transpose(1, 0, 2), conv_w, conv_b.reshape(1, cdim), _pad_lanes(dt_bias),
      _pad_lanes(a_log))
    pad_t = lambda m: jnp.pad(m, ((0, LANES - bsz), (0, 0))).T
    y, st = pl.pallas_call(
        functools.partial(_ssd_state_kernel, n_heads=n_heads),
        grid=(bsz,),
        in_specs=[pl.BlockSpec((None, width, n_state), lambda b: (b, 0, 0)),
                  _resident((width, LANES)), _resident((width, LANES)), _resident((bsz, cdim)),
                  _resident((bsz, width)), _resident((1, width)), _resident((1, width))],
        out_specs=[pl.BlockSpec((None, 1, width), lambda b: (b, 0, 0)),
                   pl.BlockSpec((None, width, n_state), lambda b: (b, 0, 0))],
        out_shape=[jax.ShapeDtypeStruct((bsz, 1, width), F32), jax.ShapeDtypeStruct(h0.shape, F32)],
        compiler_params=_params("arbitrary"),
    )(h0, pad_t(dtx), pad_t(dax), conv, z, jnp.repeat(d_skip, SSD_HEAD_DIM).reshape(1, width),
      norm_w.reshape(1, width))
    return y.reshape(bsz, width), new_conv.transpose(1, 0, 2), st


def _diff_lambda(lq1_ref, lk1_ref, lq2_ref, lk2_ref, lam_init):
    s1 = jnp.sum(lq1_ref[...] * lk1_ref[...], axis=-1, keepdims=True)
    s2 = jnp.sum(lq2_ref[...] * lk2_ref[...], axis=-1, keepdims=True)
    return jnp.exp(s1) - jnp.exp(s2) + lam_init


def _flash_kernel(q_ref, k_ref, vt_ref, lq1_ref, lk1_ref, lq2_ref, lk2_ref, g_ref, o_ref, *, lam_init):
    tq, tk, dh = ATTN_TQ, ATTN_TK, DA_HEAD_DIM
    qi = pl.program_id(2)
    q = q_ref[...]
    lane = lax.broadcasted_iota(jnp.int32, q.shape, 1)
    zero = jnp.zeros_like(q)
    qs = (jnp.where(lane < dh, q, zero), jnp.where(lane >= dh, q, zero))

    sw = ATTN_Q_STRIPE
    chains = [(c, r) for r in range(tq // sw) for c in range(2)]
    q_parts = [qs[c][r * sw:(r + 1) * sw, :] for c, r in chains]

    def absorb(blocks, carry):
        scores = []
        for j, _ in blocks:
            kb = k_ref[pl.ds(pl.multiple_of(j * tk, tk), tk), :]
            scores.append([lax.dot_general(kb, qp, (((1,), (1,)), ((), ())), preferred_element_type=F32)
                           for qp in q_parts])
        for (j, masked), block_scores in zip(blocks, scores):
            vt = vt_ref[j]
            out = []
            for (c, r), s, (m, l, acc) in zip(chains, block_scores, carry):
                if masked:
                    keys = lax.broadcasted_iota(jnp.int32, (tk, sw), 0)
                    queries = lax.broadcasted_iota(jnp.int32, (tk, sw), 1) + r * sw
                    s = jnp.where(keys <= queries, s, NEG_BIG)
                m_new = jnp.maximum(m, jnp.max(s, axis=0, keepdims=True))
                alpha = jnp.exp2(m - m_new)
                p = jnp.exp2(s - m_new)
                l = alpha * l + jnp.sum(p, axis=0, keepdims=True)
                acc = alpha * acc + jnp.dot(vt, p.astype(BF16), preferred_element_type=F32)
                out.append((m_new, l, acc))
            carry = tuple(out)
        return carry

    def finish(carry):
        l0, l1 = (jnp.concatenate([carry[i][1] for i, (c, _) in enumerate(chains) if c == cc], axis=1)
                  for cc in range(2))
        a0, a1 = (jnp.concatenate([carry[i][2] for i, (c, _) in enumerate(chains) if c == cc], axis=1)
                  for cc in range(2))
        lam = _diff_lambda(lq1_ref, lk1_ref, lq2_ref, lk2_ref, lam_init)
        o = (a0 * (1.0 / l0) - lam * (a1 * (1.0 / l1))).T
        o_ref[...] = (_rms_norm(o, g_ref[...]) * (1.0 - lam_init)).astype(o_ref.dtype)

    init = tuple((jnp.full((1, sw), NEG_BIG, F32), jnp.zeros((1, sw), F32), jnp.zeros((2 * dh, sw), F32))
                 for _ in chains)
    carry = lax.fori_loop(0, qi // 2, lambda p, c: absorb([(2 * p, False), (2 * p + 1, False)], c), init)

    @pl.when(qi % 2 == 0)
    def _():
        finish(absorb([(qi, True)], carry))

    @pl.when(qi % 2 == 1)
    def _():
        finish(absorb([(qi - 1, False), (qi, True)], carry))


def _flash(qb, kb, vt, lq1, lk1, lq2, lk2, subln_w, lam_init):
    assert ATTN_TQ == ATTN_TK == ROW_TILE
    bsz, seq, width = qb.shape
    hw = 2 * DA_HEAD_DIM
    nk = seq // ATTN_TK
    vec = lambda a: a.reshape(1, -1)
    kv = pl.BlockSpec((None, seq, hw), lambda b, h, i: (b, 0, h))
    vts = pl.BlockSpec((nk, hw, ATTN_TK), lambda b, h, i: (b, h, 0))
    qo = pl.BlockSpec((None, ATTN_TQ, hw), lambda b, h, i: (b, i, h))
    small = _resident((1, DA_HEAD_DIM))
    return pl.pallas_call(
        functools.partial(_flash_kernel, lam_init=lam_init),
        grid=(bsz, width // hw, seq // ATTN_TQ),
        in_specs=[qo, kv, vts, small, small, small, small, _resident((1, hw))],
        out_specs=qo,
        out_shape=jax.ShapeDtypeStruct((bsz, seq, width), BF16),
        compiler_params=_params("parallel", "parallel", "arbitrary"),
    )(qb, kb, vt, vec(lq1), vec(lk1), vec(lq2), vec(lk2), vec(subln_w))


def _decode_attn_kernel(pt_ref, q_ref, kn_ref, vn_ref, lq1_ref, lk1_ref, lq2_ref, lk2_ref, g_ref, *rest,
                        lam_init, n_pages):
    k_refs = rest[0:n_pages]
    v_refs = rest[n_pages:2 * n_pages]
    o_ref, m_ref, l_ref, acc_ref = rest[2 * n_pages:]
    b, s = pl.program_id(0), pl.program_id(1)
    width = q_ref.shape[1]
    n_sub = width // DA_HEAD_DIM
    hw = 2 * DA_HEAD_DIM
    rows = lax.broadcasted_iota(jnp.int32, (n_sub, width), 0)
    lanes = lax.broadcasted_iota(jnp.int32, (n_sub, width), 1)
    qrow = q_ref[pl.ds(b, 1), :]
    qm = jnp.where(lanes // DA_HEAD_DIM == rows, jnp.broadcast_to(qrow, (n_sub, width)), 0.0).astype(BF16)

    @pl.when(s == 0)
    def _():
        kn = jnp.broadcast_to(kn_ref[pl.ds(b, 1), :], (SUBLANES, width))
        m_ref[...] = _dot_nt(qm, kn)[:, 0:1]
        l_ref[...] = jnp.ones_like(l_ref)
        acc_ref[...] = jnp.broadcast_to(vn_ref[pl.ds(b, 1), :], (n_sub, width))

    n_heads = width // hw
    page = k_refs[0].shape[0] // n_heads
    head_rows = lambda ref, h: ref[pl.ds(h, page, stride=n_heads), :]
    scores = []
    for k_ref in k_refs:
        sc = _dot_nt(qm[:, 0:hw], head_rows(k_ref, 0))
        for h in range(1, n_heads):
            sc = sc + _dot_nt(qm[:, h * hw:(h + 1) * hw], head_rows(k_ref, h))
        scores.append(sc)
    m_old = m_ref[...]
    m_new = m_old
    for sc in scores:
        m_new = jnp.maximum(m_new, jnp.max(sc, axis=-1, keepdims=True))
    alpha = jnp.exp2(m_old - m_new)
    ps = [jnp.exp2(sc - m_new) for sc in scores]
    l = alpha * l_ref[...]
    for p in ps:
        l = l + jnp.sum(p, axis=-1, keepdims=True)
    for h in range(n_heads):
        acc = alpha * acc_ref[:, h * hw:(h + 1) * hw]
        for p, v_ref in zip(ps, v_refs):
            acc = acc + _dot(p, head_rows(v_ref, h))
        acc_ref[:, h * hw:(h + 1) * hw] = acc
    m_ref[...] = m_new
    l_ref[...] = l

    @pl.when(s == pl.num_programs(1) - 1)
    def _():
        lam = _diff_lambda(lq1_ref, lk1_ref, lq2_ref, lk2_ref, lam_init)
        o = acc_ref[...] / l
        own = lanes // hw == rows // 2
        coef = jnp.where(rows % 2 == 0, 1.0, -lam)
        out = jnp.sum(jnp.where(own, coef * o, 0.0), axis=0, keepdims=True)
        g = g_ref[...]
        parts = [_rms_norm(out[:, h * hw:(h + 1) * hw], g) for h in range(width // hw)]
        o_ref[...] = jnp.concatenate(parts, axis=1) * (1.0 - lam_init)


def _decode_attn(page_table, q, k_new, v_new, cache_k, cache_v, layer, lq1, lk1, lq2, lk2, subln_w, lam_init):
    bsz, width = q.shape
    n_tab = page_table.shape[1]
    page_rows = cache_k.shape[2]
    npg = DEC_PAGES_PER_STEP
    hw = 2 * DA_HEAD_DIM
    vec = lambda a: a.reshape(1, -1)
    res = lambda shape: pl.BlockSpec(shape, lambda b, s, pt: (0,) * len(shape))

    def page_spec(i):
        return pl.BlockSpec((None, None, page_rows, hw),
                            lambda b, s, pt: (layer, pt[b * n_tab + s * npg + i], 0, 0))

    pages = [page_spec(i) for i in range(npg)]
    n_sub = width // DA_HEAD_DIM
    grid_spec = pltpu.PrefetchScalarGridSpec(
        num_scalar_prefetch=1,
        grid=(bsz, n_tab // npg),
        in_specs=[res((bsz, width))] * 3 + [res((1, DA_HEAD_DIM))] * 4 + [res((1, hw))] + pages + pages,
        out_specs=pl.BlockSpec((None, 1, width), lambda b, s, pt: (b, 0, 0)),
        scratch_shapes=[pltpu.VMEM((n_sub, 1), F32), pltpu.VMEM((n_sub, 1), F32), pltpu.VMEM((n_sub, width), F32)],
    )
    out = pl.pallas_call(
        functools.partial(_decode_attn_kernel, lam_init=lam_init, n_pages=npg),
        grid_spec=grid_spec,
        out_shape=jax.ShapeDtypeStruct((bsz, 1, width), F32),
        compiler_params=_params("arbitrary", "arbitrary"),
    )(page_table.reshape(-1), q, k_new, v_new, vec(lq1), vec(lk1), vec(lq2), vec(lk2), vec(subln_w),
      *([cache_k] * npg), *([cache_v] * npg))
    return out.reshape(bsz, width)


def kernel(x_prompt, x_sample, state_s5_re, state_s5_im, state_ssd, state_conv, cache_k, cache_v, page_table, p_prompt, p_sample, ln1_g, ln1_b, ln2_g, ln2_b, ssm_w_in, s5_lam_re, s5_lam_im, s5_log_step, s5_b_re, s5_b_im, s5_c_re, s5_c_im, s5_d, s5_w_glu, s5_b_glu, ssd_conv_w, ssd_conv_b, ssd_dt_bias, ssd_a_log, ssd_d, ssd_norm_w, ssm_w_out, attn_w_qkv, attn_lq1, attn_lk1, attn_lq2, attn_lk2, attn_subln_w, attn_w_out, mlp_w1, mlp_w2, ple_w_proj, ple_w_gate):
    bp, seq_p, d_model = x_prompt.shape
    bs, seq_s, _ = x_sample.shape
    assert seq_s == 1
    depth = ln1_g.shape[0]
    alpha = (2 * depth) ** 0.25
    s5_w = s5_d.shape[1]
    s5_groups, s5_state = s5_lam_re.shape[1:]
    ns = s5_groups * s5_state
    ssd_heads = ssd_dt_bias.shape[1]
    ssd_w = ssd_heads * SSD_HEAD_DIM
    cdim = ssd_conv_w.shape[2]
    kc = ssd_conv_w.shape[1]
    n_state = state_ssd.shape[-1]
    da_w = attn_w_out.shape[1]
    da_heads = da_w // (2 * DA_HEAD_DIM)
    page = cache_k.shape[2]
    past_len = page_table.shape[1] * page
    tp = bp * seq_p

    xp = x_prompt.reshape(tp, d_model)
    xs = x_sample.reshape(bs, d_model)
    tables_p = _rope_tables(jnp.arange(seq_p))
    tables_s = _rope_tables(jnp.full((bs,), past_len))
    ck = cache_k.reshape(cache_k.shape[0], cache_k.shape[1], page * da_heads, 2 * DA_HEAD_DIM)
    cv = cache_v.reshape(cache_v.shape[0], cache_v.shape[1], page * da_heads, 2 * DA_HEAD_DIM)

    outs = {name: [] for name in ("s5p", "s5s", "ssdp", "ssds", "convp", "convs", "kp", "vp", "ks", "vs")}
    for i in range(depth):
        j = i // 2
        if i % 2 == 0:
            w_in = jnp.pad(ssm_w_in[j], ((0, 0), (0, LANES - ssd_heads))).astype(BF16)
            splits = (s5_w, ssd_w, cdim, LANES)
            a, bmat, cmat = _s5_prep(s5_lam_re[j], s5_lam_im[j], s5_log_step[j], s5_b_re[j], s5_b_im[j],
                                     s5_c_re[j], s5_c_im[j])
            wg = s5_w_glu[j].astype(BF16)
            ssd_prm = (ssd_conv_w[j], ssd_conv_b[j], ssd_dt_bias[j], ssd_a_log[j], ssd_d[j], ssd_norm_w[j])

            u, z, xbc, dt = _proj(xp, w_in, splits)
            ya, hfin = _s5_scan(u.reshape(bp, seq_p, s5_w), jnp.zeros((bp, 1, 2 * ns), F32), a, bmat, cmat,
                                s5_d[j], wg, s5_b_glu[j])
            yb, conv_p, st_p = _ssd_scan(z.reshape(bp, seq_p, ssd_w), xbc.reshape(bp, seq_p, cdim),
                                         dt.reshape(bp, seq_p, LANES), jnp.zeros((bp, kc - 1, cdim), F32),
                                         jnp.zeros((bp, ssd_w, n_state), F32), *ssd_prm)
            mixes_p = [ya.reshape(tp, s5_w), yb.reshape(tp, ssd_w)]
            outs["s5p"].append(hfin.reshape(bp, 2, s5_groups, s5_state))
            outs["ssdp"].append(st_p.reshape(bp, ssd_heads, SSD_HEAD_DIM, n_state))
            outs["convp"].append(conv_p)

            u, z, xbc, dt = _proj(xs, w_in, splits)
            h0 = jnp.concatenate([state_s5_re[j].reshape(bs, ns), state_s5_im[j].reshape(bs, ns)], axis=1)
            ya, hnew = _s5_step(u, h0, a, bmat, cmat, s5_d[j], wg, s5_b_glu[j])
            yb, conv_s, st_s = _ssd_step(z, xbc, dt, state_conv[j], state_ssd[j].reshape(bs, ssd_w, n_state),
                                         *ssd_prm)
            mixes_s = [ya, yb]
            outs["s5s"].append(hnew.reshape(bs, 2, s5_groups, s5_state))
            outs["ssds"].append(st_s.reshape(bs, ssd_heads, SSD_HEAD_DIM, n_state))
            outs["convs"].append(conv_s)
            w_out = ssm_w_out[j].astype(BF16)
            w_outs = [w_out[0:s5_w], w_out[s5_w:]]
        else:
            lam_init = 0.8 - 0.6 * math.exp(-0.3 * i)
            w_qkv = attn_w_qkv[j].astype(BF16)
            lam_prm = (attn_lq1[j], attn_lk1[j], attn_lq2[j], attn_lk2[j], attn_subln_w[j])

            k, v, qb, kb, vt = _qkv(xp, w_qkv, tables_p, True)
            shape3 = (bp, seq_p, da_w)
            o = _flash(qb.reshape(shape3), kb.reshape(shape3), vt, *lam_prm, lam_init)
            mixes_p = [o.reshape(tp, da_w)]
            outs["kp"].append(k.reshape(bp, seq_p, da_heads, 2 * DA_HEAD_DIM))
            outs["vp"].append(v.reshape(bp, seq_p, da_heads, 2 * DA_HEAD_DIM))

            k, v, qf = _qkv(xs, w_qkv, tables_s, False)
            o = _decode_attn(page_table, qf, k.reshape(bs, da_w), v.reshape(bs, da_w), ck, cv, j, *lam_prm,
                             lam_init)
            mixes_s = [o]
            outs["ks"].append(k.reshape(bs, 1, da_heads, 2 * DA_HEAD_DIM))
            outs["vs"].append(v.reshape(bs, 1, da_heads, 2 * DA_HEAD_DIM))
            w_outs = [attn_w_out[j].astype(BF16)]

        tail = (ln1_g[i], ln1_b[i], ln2_g[i], ln2_b[i], mlp_w1[i].astype(BF16), mlp_w2[i].astype(BF16),
                ple_w_proj[i].astype(BF16), ple_w_gate[i].astype(BF16), alpha)
        xp = _finish(xp, mixes_p, w_outs, p_prompt[i].reshape(tp, -1), *tail)
        xs = _finish(xs, mixes_s, w_outs, p_sample[i].reshape(bs, -1), *tail)

    s5p, s5s = jnp.stack(outs["s5p"]), jnp.stack(outs["s5s"])
    return (xp.reshape(bp, seq_p, d_model), xs.reshape(bs, 1, d_model),
            s5p[:, :, 0], s5p[:, :, 1], s5s[:, :, 0], s5s[:, :, 1],
            jnp.stack(outs["ssdp"]), jnp.stack(outs["ssds"]), jnp.stack(outs["convp"]), jnp.stack(outs["convs"]),
            jnp.stack(outs["kp"]), jnp.stack(outs["vp"]), jnp.stack(outs["ks"]), jnp.stack(outs["vs"]))
```

```python
import functools
import math

import jax
import jax.numpy as jnp
from jax import lax
from jax.experimental import pallas as pl
from jax.experimental.pallas import tpu as pltpu

F32 = jnp.float32
BF16 = jnp.bfloat16

S5_GROUP = 16
SSD_HEAD_DIM = 64
SSD_GROUPS = 2
DA_HEAD_DIM = 64
ROT_DIM = DA_HEAD_DIM // 4
ROPE_THETA = 500000.0
LN_EPS = 1e-5
RMS_EPS = 1e-5

LANES = 128
SUBLANES = 8
VMEM_LIMIT_BYTES = 56 * 1024 * 1024

ROW_TILE = 512
S5_SEGMENTS = SUBLANES
S5_SEG_LEN = 64
S5_LANE_CHUNK = 512
SSD_CHUNK = 256
ATTN_TQ = 512
ATTN_TK = 512
ATTN_Q_STRIPE = 256
ATTN_BLOCKS_PER_TRIP = 4
DEC_PAGES_PER_STEP = 8
NEG_BIG = -1e30


def _resident(shape):
    zeros = (0,) * len(shape)
    return pl.BlockSpec(shape, lambda *_: zeros, pipeline_mode=pl.Buffered(1))


def _params(*sem):
    return pltpu.CompilerParams(dimension_semantics=sem, vmem_limit_bytes=VMEM_LIMIT_BYTES)


def _dot(a, b):
    return jnp.dot(a.astype(BF16), b.astype(BF16), preferred_element_type=F32)


def _dot_nt(a, b):
    return lax.dot_general(a.astype(BF16), b.astype(BF16), (((1,), (1,)), ((), ())),
                           preferred_element_type=F32)


def _split3(x):
    hi = x.astype(BF16)
    r = x - hi.astype(F32)
    mid = r.astype(BF16)
    lo = (r - mid.astype(F32)).astype(BF16)
    return hi, mid, lo


def _dot_exact(a, b, selector):
    sel = (a, b)[selector].astype(BF16)
    pieces = _split3((b, a)[selector])
    dot = (lambda p: jnp.dot(sel, p, preferred_element_type=F32)) if selector == 0 else \
          (lambda p: jnp.dot(p, sel, preferred_element_type=F32))
    return dot(pieces[0]) + dot(pieces[1]) + dot(pieces[2])


def _sigmoid(x):
    return 0.5 * jnp.tanh(0.5 * x) + 0.5


def _silu(x):
    return x * _sigmoid(x)


def _layer_norm(x, g, b):
    mu = jnp.mean(x, -1, keepdims=True)
    xc = x - mu
    var = jnp.mean(xc * xc, -1, keepdims=True)
    return xc * lax.rsqrt(var + LN_EPS) * g + b


def _rms_norm(x, g):
    return x * lax.rsqrt(jnp.mean(x * x, -1, keepdims=True) + RMS_EPS) * g


def _row_tile(m):
    return ROW_TILE if m % ROW_TILE == 0 else m


def _proj_kernel(x_ref, *refs, splits):
    w_refs, o_refs = refs[:len(splits)], refs[len(splits):]
    xb = x_ref[...].astype(BF16)
    outs = iter(o_refs)
    for w_ref, widths in zip(w_refs, splits):
        off = 0
        for n in widths:
            next(outs)[...] = jnp.dot(xb, w_ref[:, off:off + n], preferred_element_type=F32)
            off += n


def _proj(x, ws, splits):
    m, k = x.shape
    tm = _row_tile(m)
    widths = [n for group in splits for n in group]
    return pl.pallas_call(
        functools.partial(_proj_kernel, splits=splits),
        grid=(m // tm,),
        in_specs=[pl.BlockSpec((tm, k), lambda i: (i, 0))] + [_resident(w.shape) for w in ws],
        out_specs=[pl.BlockSpec((tm, n), lambda i: (i, 0)) for n in widths],
        out_shape=[jax.ShapeDtypeStruct((m, n), F32) for n in widths],
        compiler_params=_params("parallel"),
    )(x, *ws)


def _rope(t, ca, sa, sb):
    half = ROT_DIM // 2
    outs = []
    for h in range(t.shape[1] // LANES):
        c = t[:, h * LANES:(h + 1) * LANES]
        up = pltpu.roll(c, LANES - half, 1)
        dn = pltpu.roll(c, half, 1)
        outs.append(c * ca + up * sa + dn * sb)
    return jnp.concatenate(outs, axis=1)


def _qkv_rows(x_ref, w_ref, ca_ref, sa_ref, sb_ref, k_ref, v_ref, *, width, scale):
    xb = x_ref[...].astype(BF16)
    ca, sa, sb = ca_ref[...], sa_ref[...], sb_ref[...]
    q = _rope(jnp.dot(xb, w_ref[:, 0:width], preferred_element_type=F32), ca, sa, sb) * scale
    k = _rope(jnp.dot(xb, w_ref[:, width:2 * width], preferred_element_type=F32), ca, sa, sb)
    v = jnp.dot(xb, w_ref[:, 2 * width:3 * width], preferred_element_type=F32)
    tm = xb.shape[0]
    n_heads = width // LANES
    for h in range(n_heads):
        k_ref[pl.ds(h, tm, stride=n_heads), :] = k[:, h * LANES:(h + 1) * LANES]
        v_ref[pl.ds(h, tm, stride=n_heads), :] = v[:, h * LANES:(h + 1) * LANES]
    return q, k, v


def _qkv_prompt_kernel(x_ref, w_ref, ca_ref, sa_ref, sb_ref, k_ref, v_ref, qb_ref, kb_ref, vt_ref, **kw):
    q, k, v = _qkv_rows(x_ref, w_ref, ca_ref, sa_ref, sb_ref, k_ref, v_ref, **kw)
    qb_ref[...] = q.astype(BF16)
    kb_ref[...] = k.astype(BF16)
    vt_ref[...] = v.T.astype(BF16)


def _qkv_decode_kernel(x_ref, w_ref, ca_ref, sa_ref, sb_ref, k_ref, v_ref, q_ref, **kw):
    q, _, _ = _qkv_rows(x_ref, w_ref, ca_ref, sa_ref, sb_ref, k_ref, v_ref, **kw)
    q_ref[...] = q


def _rope_tables(pos):
    half = ROT_DIM // 2
    inv = ROPE_THETA ** (-jnp.arange(0, ROT_DIM, 2, dtype=F32) / ROT_DIM)
    ang = pos.astype(F32)[:, None] * inv
    cos, sin = jnp.cos(ang), jnp.sin(ang)
    n = pos.shape[0]
    pad = jnp.zeros((n, DA_HEAD_DIM - ROT_DIM), F32)
    zero = jnp.zeros((n, half), F32)
    ca = jnp.concatenate([cos, cos, pad + 1.0], axis=1)
    sa = jnp.concatenate([-sin, zero, pad], axis=1)
    sb = jnp.concatenate([zero, sin, pad], axis=1)
    rep = LANES // DA_HEAD_DIM
    return jnp.tile(ca, (1, rep)), jnp.tile(sa, (1, rep)), jnp.tile(sb, (1, rep))


def _qkv(x, w, tables, for_flash):
    m, d = x.shape
    width = w.shape[1] // 3
    n_heads = width // LANES
    tm = _row_tile(m)
    nt = tables[0].shape[0] // tm
    row = lambda i: (i, 0)
    tab = pl.BlockSpec((tm, LANES), lambda i: (i % nt, 0))
    full = pl.BlockSpec((tm, width), row)
    native = pl.BlockSpec((tm * n_heads, LANES), row)
    native_shape = jax.ShapeDtypeStruct((m * n_heads, LANES), F32)
    if for_flash:
        body = _qkv_prompt_kernel
        out_specs = [native, native, full, full, pl.BlockSpec((None, width, tm), lambda i: (i, 0, 0))]
        out_shape = [native_shape, native_shape, jax.ShapeDtypeStruct((m, width), BF16),
                     jax.ShapeDtypeStruct((m, width), BF16), jax.ShapeDtypeStruct((m // tm, width, tm), BF16)]
    else:
        body = _qkv_decode_kernel
        out_specs = [native, native, full]
        out_shape = [native_shape, native_shape, jax.ShapeDtypeStruct((m, width), F32)]
    return pl.pallas_call(
        functools.partial(body, width=width, scale=DA_HEAD_DIM ** -0.5 * math.log2(math.e)),
        grid=(m // tm,),
        in_specs=[pl.BlockSpec((tm, d), row), _resident(w.shape), tab, tab, tab],
        out_specs=out_specs,
        out_shape=out_shape,
        compiler_params=_params("parallel"),
    )(x, w, *tables)


def _finish_kernel(*refs, n_mix, alpha, ff_chunk):
    x_ref = refs[0]
    y_refs = refs[1:1 + n_mix]
    wo_refs = refs[1 + n_mix:1 + 2 * n_mix]
    (p_ref, g1_ref, b1_ref, g2_ref, b2_ref, w1_ref, w2_ref, wp_ref, wg_ref, o_ref) = refs[1 + 2 * n_mix:]
    mix = _dot(y_refs[0][...], wo_refs[0][...])
    for y_ref, wo_ref in zip(y_refs[1:], wo_refs[1:]):
        mix = mix + _dot(y_ref[...], wo_ref[...])
    x1 = _layer_norm(alpha * x_ref[...] + mix, g1_ref[...], b1_ref[...])
    x1b = x1.astype(BF16)
    h = None
    for c in range(0, w1_ref.shape[1], ff_chunk):
        a = jnp.maximum(jnp.dot(x1b, w1_ref[:, c:c + ff_chunk], preferred_element_type=F32), 0.0)
        part = jnp.dot((a * a).astype(BF16), w2_ref[c:c + ff_chunk, :], preferred_element_type=F32)
        h = part if h is None else h + part
    x2 = _layer_norm(alpha * x1 + h, g2_ref[...], b2_ref[...])
    gate = _sigmoid(_dot(x2, wg_ref[...]))
    o_ref[...] = x2 + gate * _dot(p_ref[...], wp_ref[...])


def _finish(x, mixes, w_outs, p, g1, b1, g2, b2, w1, w2, wp, wg, alpha):
    m, d = x.shape
    tm = _row_tile(m)
    row = lambda i: (i, 0)
    vec = lambda a: a.reshape(1, d)
    in_specs = ([pl.BlockSpec((tm, d), row)]
                + [pl.BlockSpec((tm, y.shape[1]), row) for y in mixes]
                + [_resident(w.shape) for w in w_outs]
                + [pl.BlockSpec((tm, p.shape[1]), row)]
                + [_resident((1, d))] * 4
                + [_resident(w.shape) for w in (w1, w2, wp, wg)])
    return pl.pallas_call(
        functools.partial(_finish_kernel, n_mix=len(mixes), alpha=alpha, ff_chunk=1024),
        grid=(m // tm,),
        in_specs=in_specs,
        out_specs=pl.BlockSpec((tm, d), row),
        out_shape=jax.ShapeDtypeStruct((m, d), F32),
        compiler_params=_params("parallel"),
    )(x, *mixes, *w_outs, p, vec(g1), vec(b1), vec(g2), vec(b2), w1, w2, wp, wg)


def _s5_prep_kernel(lr_ref, li_ref, ls_ref, br_ref, bi_ref, ar_ref, ai_ref, bbr_ref, bbi_ref):
    lr, li = lr_ref[...], li_ref[...]
    step = jnp.exp(ls_ref[...])
    mag = jnp.exp(lr * step)
    a_re = mag * jnp.cos(li * step)
    a_im = mag * jnp.sin(li * step)
    den = lr * lr + li * li
    coef_re = ((a_re - 1.0) * lr + a_im * li) / den
    coef_im = (a_im * lr - (a_re - 1.0) * li) / den
    br, bi = br_ref[...], bi_ref[...]
    ar_ref[...] = a_re
    ai_ref[...] = a_im
    bbr_ref[...] = coef_re * br - coef_im * bi
    bbi_ref[...] = coef_re * bi + coef_im * br


def _s5_prep(lam_re, lam_im, log_step, b_re, b_im, c_re, c_im):
    g, p = lam_re.shape
    hh = b_re.shape[-1]
    a_re, a_im, bb_re, bb_im = pl.pallas_call(
        _s5_prep_kernel,
        out_shape=[jax.ShapeDtypeStruct((g, 1, p), F32)] * 2 + [jax.ShapeDtypeStruct((g, hh, p), F32)] * 2,
    )(lam_re.reshape(g, 1, p), lam_im.reshape(g, 1, p), log_step.reshape(g, 1, 1),
      b_re.transpose(0, 2, 1), b_im.transpose(0, 2, 1))
    a = jnp.concatenate([a_re.reshape(1, g * p), a_im.reshape(1, g * p)], axis=1)
    gt = LANES // hh
    nt = g // gt
    eye = jnp.eye(gt, dtype=bool)

    def blockdiag_in(bb):
        bb = bb.reshape(nt, gt, hh, p)
        return jnp.where(eye[None, :, None, :, None], bb[:, :, :, None, :], 0.0).reshape(nt, gt * hh, gt * p)

    def blockdiag_out(c):
        c = c.transpose(0, 2, 1).reshape(nt, gt, p, hh)
        return jnp.where(eye[None, :, None, :, None], c[:, :, :, None, :], 0.0).reshape(nt, gt * p, gt * hh)

    bmat = jnp.concatenate([blockdiag_in(bb_re), blockdiag_in(bb_im)], axis=2).astype(BF16)
    cmat = jnp.concatenate([blockdiag_out(c_re), -blockdiag_out(c_im)], axis=1).astype(BF16)
    return a, bmat, cmat


def _s5_in_map(u, bmat_ref, store):
    for t in range(bmat_ref.shape[0]):
        part = _dot(u[:, t * LANES:(t + 1) * LANES], bmat_ref[t])
        s = part.shape[1] // 2
        store(t * s, part[:, 0:s], part[:, s:2 * s])


def _s5_out_map(load, cmat_ref):
    ys = []
    for t in range(cmat_ref.shape[0]):
        s = cmat_ref.shape[1] // 2
        hr, hi = load(t * s, s)
        ys.append(_dot(hr, cmat_ref[t, 0:s, :]) + _dot(hi, cmat_ref[t, s:2 * s, :]))
    return jnp.concatenate(ys, axis=1)


def _s5_tail(y, u, d, wg, bg):
    y = jax.nn.gelu(y + d * u)
    return y * _sigmoid(_dot(y, wg) + bg)


def _s5_scan_kernel(u_ref, h0_ref, a_ref, bmat_ref, cmat_ref, d_ref, wg_ref, bg_ref,
                    y_ref, hout_ref, col_ref, up_ref, bu_ref, pw_ref, hin_ref, carry_ref):
    n_seg, seg = S5_SEGMENTS, S5_SEG_LEN
    ns = a_ref.shape[1] // 2
    lw = S5_LANE_CHUNK
    first_chunk = pl.program_id(1) == 0

    @pl.when(jnp.logical_and(pl.program_id(0) == 0, first_chunk))
    def _():
        ar, ai = a_ref[:, 0:ns], a_ref[:, ns:2 * ns]
        pw_ref[0:1, :] = a_ref[...]

        def body(k, c):
            pr, pi = c
            nr, ni = pr * ar - pi * ai, pr * ai + pi * ar
            pw_ref[pl.ds(k, 1), 0:ns] = nr
            pw_ref[pl.ds(k, 1), ns:2 * ns] = ni
            return nr, ni

        lax.fori_loop(1, seg, body, (ar, ai))

    @pl.when(first_chunk)
    def _():
        carry_ref[...] = h0_ref[...]

    n_col = u_ref.shape[1] // LANES
    for k in range(n_col):
        col_ref[k] = u_ref[:, k * LANES:(k + 1) * LANES]
    for t in range(seg):
        for k in range(n_col):
            up_ref[t * n_seg:(t + 1) * n_seg, k * LANES:(k + 1) * LANES] = col_ref[k, pl.ds(t, n_seg, stride=seg), :]
    def store_bu(off, re, im):
        bu_ref[:, off:off + re.shape[1]] = re
        bu_ref[:, ns + off:ns + off + im.shape[1]] = im

    _s5_in_map(up_ref[...], bmat_ref, store_bu)

    for lc in range(0, ns, lw):
        ar = jnp.broadcast_to(a_ref[:, lc:lc + lw], (n_seg, lw))
        ai = jnp.broadcast_to(a_ref[:, ns + lc:ns + lc + lw], (n_seg, lw))

        def sweep(t, c, lc=lc, ar=ar, ai=ai):
            hr, hi = c
            r0 = pl.multiple_of(t * n_seg, n_seg)
            nr = ar * hr - ai * hi + bu_ref[pl.ds(r0, n_seg), lc:lc + lw]
            ni = ar * hi + ai * hr + bu_ref[pl.ds(r0, n_seg), ns + lc:ns + lc + lw]
            bu_ref[pl.ds(r0, n_seg), lc:lc + lw] = nr
            bu_ref[pl.ds(r0, n_seg), ns + lc:ns + lc + lw] = ni
            return nr, ni

        zero = jnp.zeros((n_seg, lw), F32)
        lax.fori_loop(0, seg, sweep, (zero, zero), unroll=2)

    asr, asi = pw_ref[seg - 1:seg, 0:ns], pw_ref[seg - 1:seg, ns:2 * ns]
    cr, ci = carry_ref[:, 0:ns], carry_ref[:, ns:2 * ns]
    last = (seg - 1) * n_seg
    for s in range(n_seg):
        hin_ref[s:s + 1, 0:ns] = cr
        hin_ref[s:s + 1, ns:2 * ns] = ci
        er = bu_ref[last + s:last + s + 1, 0:ns]
        ei = bu_ref[last + s:last + s + 1, ns:2 * ns]
        cr, ci = asr * cr - asi * ci + er, asr * ci + asi * cr + ei
    carry_ref[:, 0:ns] = cr
    carry_ref[:, ns:2 * ns] = ci
    hout_ref[...] = carry_ref[...]

    for lc in range(0, ns, lw):
        hr_in = hin_ref[:, lc:lc + lw]
        hi_in = hin_ref[:, ns + lc:ns + lc + lw]

        def fix(t, c, lc=lc, hr_in=hr_in, hi_in=hi_in):
            r0 = pl.multiple_of(t * n_seg, n_seg)
            pr = jnp.broadcast_to(pw_ref[pl.ds(t, 1), lc:lc + lw], (n_seg, lw))
            pi = jnp.broadcast_to(pw_ref[pl.ds(t, 1), ns + lc:ns + lc + lw], (n_seg, lw))
            bu_ref[pl.ds(r0, n_seg), lc:lc + lw] += pr * hr_in - pi * hi_in
            bu_ref[pl.ds(r0, n_seg), ns + lc:ns + lc + lw] += pr * hi_in + pi * hr_in
            return c

        lax.fori_loop(0, seg, fix, 0, unroll=2)

    y = _s5_out_map(lambda off, s: (bu_ref[:, off:off + s], bu_ref[:, ns + off:ns + off + s]), cmat_ref)
    out = _s5_tail(y, up_ref[...], d_ref[...], wg_ref[...], bg_ref[...])
    for k in range(n_col):
        col_ref[k] = out[:, k * LANES:(k + 1) * LANES]
    for r0 in range(0, n_seg * seg, n_seg):
        s, t0 = divmod(r0, seg)
        for k in range(n_col):
            y_ref[r0:r0 + n_seg, k * LANES:(k + 1) * LANES] = col_ref[k, pl.ds(t0 * n_seg + s, n_seg, stride=n_seg), :]


def _s5_scan(u, h0, a, bmat, cmat, d, wg, bg):
    bsz, seq, w = u.shape
    q = S5_SEGMENTS * S5_SEG_LEN
    ns2 = a.shape[1]
    return pl.pallas_call(
        _s5_scan_kernel,
        grid=(bsz, seq // q),
        in_specs=[pl.BlockSpec((None, q, w), lambda b, c: (b, c, 0)),
                  pl.BlockSpec((None, 1, ns2), lambda b, c: (b, 0, 0)),
                  _resident(a.shape), _resident(bmat.shape), _resident(cmat.shape),
                  _resident((1, w)), _resident(wg.shape), _resident((1, w))],
        out_specs=[pl.BlockSpec((None, q, w), lambda b, c: (b, c, 0)),
                   pl.BlockSpec((None, 1, ns2), lambda b, c: (b, 0, 0))],
        out_shape=[jax.ShapeDtypeStruct((bsz, seq, w), F32), jax.ShapeDtypeStruct((bsz, 1, ns2), F32)],
        scratch_shapes=[pltpu.VMEM((w // LANES, q, LANES), F32), pltpu.VMEM((q, w), F32), pltpu.VMEM((q, ns2), F32),
                        pltpu.VMEM((S5_SEG_LEN, ns2), F32), pltpu.VMEM((S5_SEGMENTS, ns2), F32),
                        pltpu.VMEM((1, ns2), F32)],
        compiler_params=_params("arbitrary", "arbitrary"),
    )(u, h0, a, bmat, cmat, d.reshape(1, w), wg, bg.reshape(1, w))


def _s5_step_kernel(u_ref, h0_ref, a_ref, bmat_ref, cmat_ref, d_ref, wg_ref, bg_ref, y_ref, hout_ref):
    ns = a_ref.shape[1] // 2
    u = u_ref[...]

    def advance(off, bu_re, bu_im):
        s = bu_re.shape[1]
        ar, ai = a_ref[:, off:off + s], a_ref[:, ns + off:ns + off + s]
        h0r, h0i = h0_ref[:, off:off + s], h0_ref[:, ns + off:ns + off + s]
        hout_ref[:, off:off + s] = ar * h0r - ai * h0i + bu_re
        hout_ref[:, ns + off:ns + off + s] = ar * h0i + ai * h0r + bu_im

    _s5_in_map(u, bmat_ref, advance)
    y = _s5_out_map(lambda off, s: (hout_ref[:, off:off + s], hout_ref[:, ns + off:ns + off + s]), cmat_ref)
    y_ref[...] = _s5_tail(y, u, d_ref[...], wg_ref[...], bg_ref[...])


def _s5_step(u, h0, a, bmat, cmat, d, wg, bg):
    bsz, w = u.shape
    return pl.pallas_call(
        _s5_step_kernel,
        out_shape=[jax.ShapeDtypeStruct((bsz, w), F32), jax.ShapeDtypeStruct(h0.shape, F32)],
        compiler_params=pltpu.CompilerParams(vmem_limit_bytes=VMEM_LIMIT_BYTES),
    )(u, h0, a, bmat, cmat, d.reshape(1, w), wg, bg.reshape(1, w))


def _head_expander(n_heads):
    rows = lax.broadcasted_iota(jnp.int32, (LANES, n_heads * SSD_HEAD_DIM), 0)
    cols = lax.broadcasted_iota(jnp.int32, (LANES, n_heads * SSD_HEAD_DIM), 1)
    return (cols // SSD_HEAD_DIM == rows).astype(F32)


def _ssd_scan_kernel(z_ref, xbc_ref, dt_ref, cs_ref, h0_ref, cw_ref, cb_ref, dtb_ref, alog_ref, dx_ref, nw_ref,
                     y_ref, cso_ref, ho_ref, xpad_ref, st_ref, *, n_heads):
    q = z_ref.shape[0]
    width = n_heads * SSD_HEAD_DIM
    n_state = (xbc_ref.shape[1] - width) // (2 * SSD_GROUPS)
    kc = cw_ref.shape[0]
    hg = n_heads // SSD_GROUPS
    gw = hg * SSD_HEAD_DIM
    c = pl.program_id(1)
    tail0 = SUBLANES - (kc - 1)

    @pl.when(c == 0)
    def _():
        xpad_ref[tail0:SUBLANES, :] = cs_ref[...]
        st_ref[...] = h0_ref[...].T

    @pl.when(c != 0)
    def _():
        xpad_ref[tail0:SUBLANES, :] = xpad_ref[q + tail0:q + SUBLANES, :]

    xbc = xbc_ref[...]
    xpad_ref[SUBLANES:SUBLANES + q, :] = xbc
    cso_ref[...] = xbc[q - (kc - 1):q, :]

    conv = cb_ref[...] + xbc * cw_ref[kc - 1:kc, :]
    for k in range(kc - 1):
        conv = conv + xpad_ref[tail0 + k:tail0 + k + q, :] * cw_ref[k:k + 1, :]
    conv = _silu(conv)
    xs = conv[:, 0:width]
    bm = [conv[:, width + g * n_state:width + (g + 1) * n_state] for g in range(SSD_GROUPS)]
    cm = [conv[:, width + (SSD_GROUPS + g) * n_state:width + (SSD_GROUPS + g + 1) * n_state]
          for g in range(SSD_GROUPS)]

    dt = jax.nn.softplus(dt_ref[...] + dtb_ref[...])
    av = dt * (-jnp.exp(alog_ref[...]))
    rows = lax.broadcasted_iota(jnp.int32, (q, q), 0)
    cols = lax.broadcasted_iota(jnp.int32, (q, q), 1)
    causal = rows >= cols
    a_cs = _dot_exact(causal.astype(F32), av, 0)
    a_cs_t = a_cs.T
    expand = _head_expander(n_heads)
    a_cs_x = _dot_exact(a_cs, expand, 1)
    tot_x = a_cs_x[q - 1:q, :]
    xdt = xs * _dot_exact(dt, expand, 1)
    cb = [_dot_nt(cm[g], bm[g]) for g in range(SSD_GROUPS)]

    y_parts = []
    for h in range(n_heads):
        g = h // hg
        seg = jnp.where(causal, a_cs[:, h:h + 1] - a_cs_t[h:h + 1, :], -jnp.inf)
        gmat = cb[g] * jnp.exp(seg)
        y_parts.append(_dot(gmat, xdt[:, h * SSD_HEAD_DIM:(h + 1) * SSD_HEAD_DIM]))
    y = jnp.concatenate(y_parts, axis=1)

    st = st_ref[...]
    y_off = jnp.concatenate([_dot(cm[g], st[:, g * gw:(g + 1) * gw]) for g in range(SSD_GROUPS)], axis=1)
    y = y + y_off * jnp.exp(a_cs_x) + dx_ref[...] * xs
    xdec = xdt * jnp.exp(tot_x - a_cs_x)
    upd = jnp.concatenate([_dot(bm[g].T, xdec[:, g * gw:(g + 1) * gw]) for g in range(SSD_GROUPS)], axis=1)
    st_new = st * jnp.exp(tot_x) + upd
    st_ref[...] = st_new

    @pl.when(c == pl.num_programs(1) - 1)
    def _():
        ho_ref[...] = st_new.T

    z = z_ref[...]
    y_ref[...] = _rms_norm(y * _silu(z), nw_ref[...]).astype(y_ref.dtype)


def _pad_lanes(v):
    return jnp.pad(v.reshape(1, -1), ((0, 0), (0, LANES - v.shape[-1])))


def _ssd_scan(z, xbc, dt, conv_state, h0, conv_w, conv_b, dt_bias, a_log, d_skip, norm_w):
    bsz, seq, width = z.shape
    cdim = xbc.shape[2]
    n_heads = dt_bias.shape[0]
    kc = conv_w.shape[0]
    n_state = h0.shape[2]
    q = SSD_CHUNK
    blk = lambda n: pl.BlockSpec((None, q, n), lambda b, c: (b, c, 0))
    per_b = lambda s: pl.BlockSpec((None,) + s, lambda b, c: (b, 0, 0))
    return pl.pallas_call(
        functools.partial(_ssd_scan_kernel, n_heads=n_heads),
        grid=(bsz, seq // q),
        in_specs=[blk(width), blk(cdim), blk(LANES), per_b((kc - 1, cdim)), per_b((width, n_state)),
                  _resident((kc, cdim)), _resident((1, cdim)), _resident((1, LANES)), _resident((1, LANES)),
                  _resident((1, width)), _resident((1, width))],
        out_specs=[blk(width), per_b((kc - 1, cdim)), per_b((width, n_state))],
        out_shape=[jax.ShapeDtypeStruct((bsz, seq, width), BF16),
                   jax.ShapeDtypeStruct((bsz, kc - 1, cdim), F32),
                   jax.ShapeDtypeStruct((bsz, width, n_state), F32)],
        scratch_shapes=[pltpu.VMEM((q + SUBLANES, cdim), F32), pltpu.VMEM((n_state, width), F32)],
        compiler_params=_params("arbitrary", "arbitrary"),
    )(z, xbc, dt, conv_state, h0, conv_w, conv_b.reshape(1, cdim), _pad_lanes(dt_bias), _pad_lanes(a_log),
      jnp.repeat(d_skip, SSD_HEAD_DIM).reshape(1, width), norm_w.reshape(1, width))


def _ssd_pre_kernel(xbc_ref, dt_ref, cs_ref, cw_ref, cb_ref, dtb_ref, alog_ref,
                    conv_ref, cso_ref, dtx_ref, dax_ref, *, n_heads):
    kc = cw_ref.shape[0]
    width = n_heads * SSD_HEAD_DIM
    xbc = xbc_ref[...]
    conv = cb_ref[...] + xbc * cw_ref[kc - 1:kc, :]
    for k in range(kc - 1):
        conv = conv + cs_ref[k] * cw_ref[k:k + 1, :]
    for k in range(kc - 2):
        cso_ref[k] = cs_ref[k + 1]
    cso_ref[kc - 2] = xbc
    conv = _silu(conv)
    conv_ref[...] = conv
    dt = jax.nn.softplus(dt_ref[...] + dtb_ref[...])
    expand = _head_expander(n_heads)
    dtx_ref[...] = conv[:, 0:width] * _dot_exact(dt, expand, 1)
    dax_ref[...] = jnp.exp(_dot_exact(dt * (-jnp.exp(alog_ref[...])), expand, 1))


def _ssd_state_kernel(st_ref, dtxt_ref, daxt_ref, conv_ref, z_ref, dx_ref, nw_ref, y_ref, sto_ref, *, n_heads):
    b = pl.program_id(0)
    width, n_state = st_ref.shape
    gw = width // SSD_GROUPS
    nb = dtxt_ref.shape[1]
    onehot = (lax.broadcasted_iota(jnp.int32, (nb, n_state), 0) == b).astype(F32)
    dtx_col = _dot_exact(dtxt_ref[...], onehot, 1)
    da_col = _dot_exact(daxt_ref[...], onehot, 1)
    row = conv_ref[pl.ds(b, 1), :]
    ys = []
    for g in range(SSD_GROUPS):
        r = slice(g * gw, (g + 1) * gw)
        bm = row[:, width + g * n_state:width + (g + 1) * n_state]
        cm = row[:, width + (SSD_GROUPS + g) * n_state:width + (SSD_GROUPS + g + 1) * n_state]
        new = st_ref[r, :] * da_col[r, :] + dtx_col[r, :] * bm
        sto_ref[r, :] = new
        ys.append(_dot_nt(jnp.broadcast_to(cm, (SUBLANES, n_state)), new)[0:1, :])
    xs = row[:, 0:width]
    y = jnp.concatenate(ys, axis=1) + dx_ref[...] * xs
    y_ref[...] = _rms_norm(y * _silu(z_ref[pl.ds(b, 1), :]), nw_ref[...])


def _ssd_step(z, xbc, dt, conv_state, h0, conv_w, conv_b, dt_bias, a_log, d_skip, norm_w):
    bsz, width = z.shape
    cdim = xbc.shape[1]
    n_heads = dt_bias.shape[0]
    kc = conv_w.shape[0]
    n_state = h0.shape[2]
    conv, new_conv, dtx, dax = pl.pallas_call(
        functools.partial(_ssd_pre_kernel, n_heads=n_heads),
        out_shape=[jax.ShapeDtypeStruct((bsz, cdim), F32), jax.ShapeDtypeStruct((kc - 1, bsz, cdim), F32),
                   jax.ShapeDtypeStruct((bsz, width), F32), jax.ShapeDtypeStruct((bsz, width), F32)],
        compiler_params=pltpu.CompilerParams(vmem_limit_bytes=VMEM_LIMIT_BYTES),
    )(xbc, dt, conv_state.transpose(1, 0, 2), conv_w, conv_b.reshape(1, cdim), _pad_lanes(dt_bias),
      _pad_lanes(a_log))
    pad_t = lambda m: jnp.pad(m, ((0, LANES - bsz), (0, 0))).T
    y, st = pl.pallas_call(
        functools.partial(_ssd_state_kernel, n_heads=n_heads),
        grid=(bsz,),
        in_specs=[pl.BlockSpec((None, width, n_state), lambda b: (b, 0, 0)),
                  _resident((width, LANES)), _resident((width, LANES)), _resident((bsz, cdim)),
                  _resident((bsz, width)), _resident((1, width)), _resident((1, width))],
        out_specs=[pl.BlockSpec((None, 1, width), lambda b: (b, 0, 0)),
                   pl.BlockSpec((None, width, n_state), lambda b: (b, 0, 0))],
        out_shape=[jax.ShapeDtypeStruct((bsz, 1, width), F32), jax.ShapeDtypeStruct(h0.shape, F32)],
        compiler_params=_params("arbitrary"),
    )(h0, pad_t(dtx), pad_t(dax), conv, z, jnp.repeat(d_skip, SSD_HEAD_DIM).reshape(1, width),
      norm_w.reshape(1, width))
    return y.reshape(bsz, width), new_conv.transpose(1, 0, 2), st


def _diff_lambda(lq1_ref, lk1_ref, lq2_ref, lk2_ref, lam_init):
    s1 = jnp.sum(lq1_ref[...] * lk1_ref[...], axis=-1, keepdims=True)
    s2 = jnp.sum(lq2_ref[...] * lk2_ref[...], axis=-1, keepdims=True)
    return jnp.exp(s1) - jnp.exp(s2) + lam_init


def _flash_kernel(q_ref, k_ref, vt_ref, lq1_ref, lk1_ref, lq2_ref, lk2_ref, g_ref, o_ref, *, lam_init):
    tq, tk, dh = ATTN_TQ, ATTN_TK, DA_HEAD_DIM
    qi = pl.program_id(2)
    q = q_ref[...]
    lane = lax.broadcasted_iota(jnp.int32, q.shape, 1)
    zero = jnp.zeros_like(q)
    qs = (jnp.where(lane < dh, q, zero), jnp.where(lane >= dh, q, zero))

    sw = ATTN_Q_STRIPE
    chains = [(c, r) for r in range(tq // sw) for c in range(2)]
    q_parts = [qs[c][r * sw:(r + 1) * sw, :] for c, r in chains]

    def absorb(blocks, carry):
        scores = []
        for j, _ in blocks:
            kb = k_ref[pl.ds(pl.multiple_of(j * tk, tk), tk), :]
            scores.append([lax.dot_general(kb, qp, (((1,), (1,)), ((), ())), preferred_element_type=F32)
                           for qp in q_parts])
        for (j, masked), block_scores in zip(blocks, scores):
            vt = vt_ref[j]
            out = []
            for (c, r), s, (m, l, acc) in zip(chains, block_scores, carry):
                if masked:
                    keys = lax.broadcasted_iota(jnp.int32, (tk, sw), 0)
                    queries = lax.broadcasted_iota(jnp.int32, (tk, sw), 1) + r * sw
                    s = jnp.where(keys <= queries, s, NEG_BIG)
                m_new = jnp.maximum(m, jnp.max(s, axis=0, keepdims=True))
                alpha = jnp.exp2(m - m_new)
                p = jnp.exp2(s - m_new)
                l = alpha * l + jnp.sum(p, axis=0, keepdims=True)
                acc = alpha * acc + jnp.dot(vt, p.astype(BF16), preferred_element_type=F32)
                out.append((m_new, l, acc))
            carry = tuple(out)
        return carry

    def finish(carry):
        l0, l1 = (jnp.concatenate([carry[i][1] for i, (c, _) in enumerate(chains) if c == cc], axis=1)
                  for cc in range(2))
        a0, a1 = (jnp.concatenate([carry[i][2] for i, (c, _) in enumerate(chains) if c == cc], axis=1)
                  for cc in range(2))
        lam = _diff_lambda(lq1_ref, lk1_ref, lq2_ref, lk2_ref, lam_init)
        o = (a0 * (1.0 / l0) - lam * (a1 * (1.0 / l1))).T
        o_ref[...] = (_rms_norm(o, g_ref[...]) * (1.0 - lam_init)).astype(o_ref.dtype)

    n = ATTN_BLOCKS_PER_TRIP
    init = tuple((jnp.full((1, sw), NEG_BIG, F32), jnp.zeros((1, sw), F32), jnp.zeros((2 * dh, sw), F32))
                 for _ in chains)
    carry = lax.fori_loop(0, qi // n, lambda p, c: absorb([(n * p + i, False) for i in range(n)], c), init)

    for rem in range(n):
        @pl.when(qi % n == rem)
        def _(rem=rem):
            finish(absorb([(qi - rem + i, False) for i in range(rem)] + [(qi, True)], carry))


def _flash(qb, kb, vt, lq1, lk1, lq2, lk2, subln_w, lam_init):
    assert ATTN_TQ == ATTN_TK == ROW_TILE
    bsz, seq, width = qb.shape
    hw = 2 * DA_HEAD_DIM
    nk = seq // ATTN_TK
    vec = lambda a: a.reshape(1, -1)
    kv = pl.BlockSpec((None, seq, hw), lambda b, h, i: (b, 0, h))
    vts = pl.BlockSpec((nk, hw, ATTN_TK), lambda b, h, i: (b, h, 0))
    qo = pl.BlockSpec((None, ATTN_TQ, hw), lambda b, h, i: (b, i, h))
    small = _resident((1, DA_HEAD_DIM))
    return pl.pallas_call(
        functools.partial(_flash_kernel, lam_init=lam_init),
        grid=(bsz, width // hw, seq // ATTN_TQ),
        in_specs=[qo, kv, vts, small, small, small, small, _resident((1, hw))],
        out_specs=qo,
        out_shape=jax.ShapeDtypeStruct((bsz, seq, width), BF16),
        compiler_params=_params("parallel", "parallel", "arbitrary"),
    )(qb, kb, vt, vec(lq1), vec(lk1), vec(lq2), vec(lk2), vec(subln_w))


def _decode_attn_kernel(pt_ref, q_ref, kn_ref, vn_ref, lq1_ref, lk1_ref, lq2_ref, lk2_ref, g_ref, *rest,
                        lam_init, n_pages):
    k_refs = rest[0:n_pages]
    v_refs = rest[n_pages:2 * n_pages]
    o_ref, m_ref, l_ref, acc_ref = rest[2 * n_pages:]
    b, s = pl.program_id(0), pl.program_id(1)
    width = q_ref.shape[1]
    n_sub = width // DA_HEAD_DIM
    hw = 2 * DA_HEAD_DIM
    rows = lax.broadcasted_iota(jnp.int32, (n_sub, width), 0)
    lanes = lax.broadcasted_iota(jnp.int32, (n_sub, width), 1)
    qrow = q_ref[pl.ds(b, 1), :]
    qm = jnp.where(lanes // DA_HEAD_DIM == rows, jnp.broadcast_to(qrow, (n_sub, width)), 0.0).astype(BF16)

    @pl.when(s == 0)
    def _():
        kn = jnp.broadcast_to(kn_ref[pl.ds(b, 1), :], (SUBLANES, width))
        m_ref[...] = _dot_nt(qm, kn)[:, 0:1]
        l_ref[...] = jnp.ones_like(l_ref)
        acc_ref[...] = jnp.broadcast_to(vn_ref[pl.ds(b, 1), :], (n_sub, width))

    n_heads = width // hw
    page = k_refs[0].shape[0] // n_heads
    def by_position(ref):
        return jnp.concatenate([ref[pl.ds(h, page, stride=n_heads), :].astype(BF16) for h in range(n_heads)], axis=1)

    scores = [_dot_nt(qm, by_position(k_ref)) for k_ref in k_refs]
    m_old = m_ref[...]
    m_new = m_old
    for sc in scores:
        m_new = jnp.maximum(m_new, jnp.max(sc, axis=-1, keepdims=True))
    alpha = jnp.exp2(m_old - m_new)
    l = alpha * l_ref[...]
    acc = alpha * acc_ref[...]
    for sc, v_ref in zip(scores, v_refs):
        p = jnp.exp2(sc - m_new)
        l = l + jnp.sum(p, axis=-1, keepdims=True)
        acc = acc + _dot(p, by_position(v_ref))
    acc_ref[...] = acc
    m_ref[...] = m_new
    l_ref[...] = l

    @pl.when(s == pl.num_programs(1) - 1)
    def _():
        lam = _diff_lambda(lq1_ref, lk1_ref, lq2_ref, lk2_ref, lam_init)
        o = acc_ref[...] / l
        own = lanes // hw == rows // 2
        coef = jnp.where(rows % 2 == 0, 1.0, -lam)
        out = jnp.sum(jnp.where(own, coef * o, 0.0), axis=0, keepdims=True)
        g = g_ref[...]
        parts = [_rms_norm(out[:, h * hw:(h + 1) * hw], g) for h in range(width // hw)]
        o_ref[...] = jnp.concatenate(parts, axis=1) * (1.0 - lam_init)


def _decode_attn(page_table, q, k_new, v_new, cache_k, cache_v, layer, lq1, lk1, lq2, lk2, subln_w, lam_init):
    bsz, width = q.shape
    n_tab = page_table.shape[1]
    page_rows = cache_k.shape[2]
    npg = DEC_PAGES_PER_STEP
    hw = 2 * DA_HEAD_DIM
    vec = lambda a: a.reshape(1, -1)
    res = lambda shape: pl.BlockSpec(shape, lambda b, s, pt: (0,) * len(shape))

    def page_spec(i):
        return pl.BlockSpec((None, None, page_rows, hw),
                            lambda b, s, pt: (layer, pt[b * n_tab + s * npg + i], 0, 0))

    pages = [page_spec(i) for i in range(npg)]
    n_sub = width // DA_HEAD_DIM
    grid_spec = pltpu.PrefetchScalarGridSpec(
        num_scalar_prefetch=1,
        grid=(bsz, n_tab // npg),
        in_specs=[res((bsz, width))] * 3 + [res((1, DA_HEAD_DIM))] * 4 + [res((1, hw))] + pages + pages,
        out_specs=pl.BlockSpec((None, 1, width), lambda b, s, pt: (b, 0, 0)),
        scratch_shapes=[pltpu.VMEM((n_sub, 1), F32), pltpu.VMEM((n_sub, 1), F32), pltpu.VMEM((n_sub, width), F32)],
    )
    out = pl.pallas_call(
        functools.partial(_decode_attn_kernel, lam_init=lam_init, n_pages=npg),
        grid_spec=grid_spec,
        out_shape=jax.ShapeDtypeStruct((bsz, 1, width), F32),
        compiler_params=_params("arbitrary", "arbitrary"),
    )(page_table.reshape(-1), q, k_new, v_new, vec(lq1), vec(lk1), vec(lq2), vec(lk2), vec(subln_w),
      *([cache_k] * npg), *([cache_v] * npg))
    return out.reshape(bsz, width)


def kernel(x_prompt, x_sample, state_s5_re, state_s5_im, state_ssd, state_conv, cache_k, cache_v, page_table, p_prompt, p_sample, ln1_g, ln1_b, ln2_g, ln2_b, ssm_w_in, s5_lam_re, s5_lam_im, s5_log_step, s5_b_re, s5_b_im, s5_c_re, s5_c_im, s5_d, s5_w_glu, s5_b_glu, ssd_conv_w, ssd_conv_b, ssd_dt_bias, ssd_a_log, ssd_d, ssd_norm_w, ssm_w_out, attn_w_qkv, attn_lq1, attn_lk1, attn_lq2, attn_lk2, attn_subln_w, attn_w_out, mlp_w1, mlp_w2, ple_w_proj, ple_w_gate):
    bp, seq_p, d_model = x_prompt.shape
    bs, seq_s, _ = x_sample.shape
    assert seq_s == 1
    depth = ln1_g.shape[0]
    alpha = (2 * depth) ** 0.25
    s5_w = s5_d.shape[1]
    s5_groups, s5_state = s5_lam_re.shape[1:]
    ns = s5_groups * s5_state
    ssd_heads = ssd_dt_bias.shape[1]
    ssd_w = ssd_heads * SSD_HEAD_DIM
    cdim = ssd_conv_w.shape[2]
    kc = ssd_conv_w.shape[1]
    n_state = state_ssd.shape[-1]
    da_w = attn_w_out.shape[1]
    da_heads = da_w // (2 * DA_HEAD_DIM)
    page = cache_k.shape[2]
    past_len = page_table.shape[1] * page
    tp = bp * seq_p

    xp = x_prompt.reshape(tp, d_model)
    xs = x_sample.reshape(bs, d_model)
    tables_p = _rope_tables(jnp.arange(seq_p))
    tables_s = _rope_tables(jnp.full((bs,), past_len))
    ck = cache_k.reshape(cache_k.shape[0], cache_k.shape[1], page * da_heads, 2 * DA_HEAD_DIM)
    cv = cache_v.reshape(cache_v.shape[0], cache_v.shape[1], page * da_heads, 2 * DA_HEAD_DIM)

    outs = {name: [] for name in ("s5p", "s5s", "ssdp", "ssds", "convp", "convs", "kp", "vp", "ks", "vs")}
    for i in range(depth):
        j = i // 2
        if i % 2 == 0:
            n_main = s5_w + ssd_w + cdim
            w_in = [ssm_w_in[j][:, :n_main].astype(BF16),
                    jnp.pad(ssm_w_in[j][:, n_main:], ((0, 0), (0, LANES - ssd_heads))).astype(BF16)]
            splits = ((s5_w, ssd_w, cdim), (LANES,))
            a, bmat, cmat = _s5_prep(s5_lam_re[j], s5_lam_im[j], s5_log_step[j], s5_b_re[j], s5_b_im[j],
                                     s5_c_re[j], s5_c_im[j])
            wg = s5_w_glu[j].astype(BF16)
            ssd_prm = (ssd_conv_w[j], ssd_conv_b[j], ssd_dt_bias[j], ssd_a_log[j], ssd_d[j], ssd_norm_w[j])

            u, z, xbc, dt = _proj(xp, w_in, splits)
            ya, hfin = _s5_scan(u.reshape(bp, seq_p, s5_w), jnp.zeros((bp, 1, 2 * ns), F32), a, bmat, cmat,
                                s5_d[j], wg, s5_b_glu[j])
            yb, conv_p, st_p = _ssd_scan(z.reshape(bp, seq_p, ssd_w), xbc.reshape(bp, seq_p, cdim),
                                         dt.reshape(bp, seq_p, LANES), jnp.zeros((bp, kc - 1, cdim), F32),
                                         jnp.zeros((bp, ssd_w, n_state), F32), *ssd_prm)
            mixes_p = [ya.reshape(tp, s5_w), yb.reshape(tp, ssd_w)]
            outs["s5p"].append(hfin.reshape(bp, 2, s5_groups, s5_state))
            outs["ssdp"].append(st_p.reshape(bp, ssd_heads, SSD_HEAD_DIM, n_state))
            outs["convp"].append(conv_p)

            u, z, xbc, dt = _proj(xs, w_in, splits)
            h0 = jnp.concatenate([state_s5_re[j].reshape(bs, ns), state_s5_im[j].reshape(bs, ns)], axis=1)
            ya, hnew = _s5_step(u, h0, a, bmat, cmat, s5_d[j], wg, s5_b_glu[j])
            yb, conv_s, st_s = _ssd_step(z, xbc, dt, state_conv[j], state_ssd[j].reshape(bs, ssd_w, n_state),
                                         *ssd_prm)
            mixes_s = [ya, yb]
            outs["s5s"].append(hnew.reshape(bs, 2, s5_groups, s5_state))
            outs["ssds"].append(st_s.reshape(bs, ssd_heads, SSD_HEAD_DIM, n_state))
            outs["convs"].append(conv_s)
            w_out = ssm_w_out[j].astype(BF16)
            w_outs = [w_out[0:s5_w], w_out[s5_w:]]
        else:
            lam_init = 0.8 - 0.6 * math.exp(-0.3 * i)
            w_qkv = attn_w_qkv[j].astype(BF16)
            lam_prm = (attn_lq1[j], attn_lk1[j], attn_lq2[j], attn_lk2[j], attn_subln_w[j])

            k, v, qb, kb, vt = _qkv(xp, w_qkv, tables_p, True)
            shape3 = (bp, seq_p, da_w)
            o = _flash(qb.reshape(shape3), kb.reshape(shape3), vt, *lam_prm, lam_init)
            mixes_p = [o.reshape(tp, da_w)]
            outs["kp"].append(k.reshape(bp, seq_p, da_heads, 2 * DA_HEAD_DIM))
            outs["vp"].append(v.reshape(bp, seq_p, da_heads, 2 * DA_HEAD_DIM))

            k, v, qf = _qkv(xs, w_qkv, tables_s, False)
            o = _decode_attn(page_table, qf, k.reshape(bs, da_w), v.reshape(bs, da_w), ck, cv, j, *lam_prm,
                             lam_init)
            mixes_s = [o]
            outs["ks"].append(k.reshape(bs, 1, da_heads, 2 * DA_HEAD_DIM))
            outs["vs"].append(v.reshape(bs, 1, da_heads, 2 * DA_HEAD_DIM))
            w_outs = [attn_w_out[j].astype(BF16)]

        tail = (ln1_g[i], ln1_b[i], ln2_g[i], ln2_b[i], mlp_w1[i].astype(BF16), mlp_w2[i].astype(BF16),
                ple_w_proj[i].astype(BF16), ple_w_gate[i].astype(BF16), alpha)
        xp = _finish(xp, mixes_p, w_outs, p_prompt[i].reshape(tp, -1), *tail)
        xs = _finish(xs, mixes_s, w_outs, p_sample[i].reshape(bs, -1), *tail)

    s5p, s5s = jnp.stack(outs["s5p"]), jnp.stack(outs["s5s"])
    return (xp.reshape(bp, seq_p, d_model), xs.reshape(bs, 1, d_model),
            s5p[:, :, 0], s5p[:, :, 1], s5s[:, :, 0], s5s[:, :, 1],
            jnp.stack(outs["ssdp"]), jnp.stack(outs["ssds"]), jnp.stack(outs["convp"]), jnp.stack(outs["convs"]),
            jnp.stack(outs["kp"]), jnp.stack(outs["vp"]), jnp.stack(outs["ks"]), jnp.stack(outs["vs"]))
```

```python
import functools
import math

import jax
import jax.numpy as jnp
from jax import lax
from jax.experimental import pallas as pl
from jax.experimental.pallas import tpu as pltpu

F32 = jnp.float32
BF16 = jnp.bfloat16

S5_GROUP = 16
SSD_HEAD_DIM = 64
SSD_GROUPS = 2
DA_HEAD_DIM = 64
ROT_DIM = DA_HEAD_DIM // 4
ROPE_THETA = 500000.0
LN_EPS = 1e-5
RMS_EPS = 1e-5

LANES = 128
SUBLANES = 8
VMEM_LIMIT_BYTES = 56 * 1024 * 1024

ROW_TILE = 512
S5_SEGMENTS = SUBLANES
S5_SEG_LEN = 64
S5_LANE_CHUNK = 512
SSD_CHUNK = 256
ATTN_TQ = 512
ATTN_TK = 512
ATTN_Q_STRIPE = 256
ATTN_BLOCKS_PER_TRIP = 4
DEC_PAGES_PER_STEP = 16
NEG_BIG = -1e30


def _resident(shape):
    zeros = (0,) * len(shape)
    return pl.BlockSpec(shape, lambda *_: zeros, pipeline_mode=pl.Buffered(1))


def _params(*sem):
    return pltpu.CompilerParams(dimension_semantics=sem, vmem_limit_bytes=VMEM_LIMIT_BYTES)


def _dot(a, b):
    return jnp.dot(a.astype(BF16), b.astype(BF16), preferred_element_type=F32)


def _dot_nt(a, b):
    return lax.dot_general(a.astype(BF16), b.astype(BF16), (((1,), (1,)), ((), ())),
                           preferred_element_type=F32)


def _split3(x):
    hi = x.astype(BF16)
    r = x - hi.astype(F32)
    mid = r.astype(BF16)
    lo = (r - mid.astype(F32)).astype(BF16)
    return hi, mid, lo


def _dot_exact(a, b, selector):
    sel = (a, b)[selector].astype(BF16)
    pieces = _split3((b, a)[selector])
    dot = (lambda p: jnp.dot(sel, p, preferred_element_type=F32)) if selector == 0 else \
          (lambda p: jnp.dot(p, sel, preferred_element_type=F32))
    return dot(pieces[0]) + dot(pieces[1]) + dot(pieces[2])


def _sigmoid(x):
    return 0.5 * jnp.tanh(0.5 * x) + 0.5


def _silu(x):
    return x * _sigmoid(x)


def _layer_norm(x, g, b):
    mu = jnp.mean(x, -1, keepdims=True)
    xc = x - mu
    var = jnp.mean(xc * xc, -1, keepdims=True)
    return xc * lax.rsqrt(var + LN_EPS) * g + b


def _rms_norm(x, g):
    return x * lax.rsqrt(jnp.mean(x * x, -1, keepdims=True) + RMS_EPS) * g


def _row_tile(m):
    return ROW_TILE if m % ROW_TILE == 0 else m


def _proj_kernel(x_ref, *refs, splits):
    w_refs, o_refs = refs[:len(splits)], refs[len(splits):]
    xb = x_ref[...].astype(BF16)
    outs = iter(o_refs)
    for w_ref, widths in zip(w_refs, splits):
        off = 0
        for n in widths:
            next(outs)[...] = jnp.dot(xb, w_ref[:, off:off + n], preferred_element_type=F32)
            off += n


def _proj(x, ws, splits):
    m, k = x.shape
    tm = _row_tile(m)
    widths = [n for group in splits for n in group]
    return pl.pallas_call(
        functools.partial(_proj_kernel, splits=splits),
        grid=(m // tm,),
        in_specs=[pl.BlockSpec((tm, k), lambda i: (i, 0))] + [_resident(w.shape) for w in ws],
        out_specs=[pl.BlockSpec((tm, n), lambda i: (i, 0)) for n in widths],
        out_shape=[jax.ShapeDtypeStruct((m, n), F32) for n in widths],
        compiler_params=_params("parallel"),
    )(x, *ws)


def _rope(t, ca, sa, sb):
    half = ROT_DIM // 2
    outs = []
    for h in range(t.shape[1] // LANES):
        c = t[:, h * LANES:(h + 1) * LANES]
        up = pltpu.roll(c, LANES - half, 1)
        dn = pltpu.roll(c, half, 1)
        outs.append(c * ca + up * sa + dn * sb)
    return jnp.concatenate(outs, axis=1)


def _qkv_rows(x_ref, w_ref, ca_ref, sa_ref, sb_ref, k_ref, v_ref, *, width, scale):
    xb = x_ref[...].astype(BF16)
    ca, sa, sb = ca_ref[...], sa_ref[...], sb_ref[...]
    q = _rope(jnp.dot(xb, w_ref[:, 0:width], preferred_element_type=F32), ca, sa, sb) * scale
    k = _rope(jnp.dot(xb, w_ref[:, width:2 * width], preferred_element_type=F32), ca, sa, sb)
    v = jnp.dot(xb, w_ref[:, 2 * width:3 * width], preferred_element_type=F32)
    tm = xb.shape[0]
    n_heads = width // LANES
    for h in range(n_heads):
        k_ref[pl.ds(h, tm, stride=n_heads), :] = k[:, h * LANES:(h + 1) * LANES]
        v_ref[pl.ds(h, tm, stride=n_heads), :] = v[:, h * LANES:(h + 1) * LANES]
    return q, k, v


def _qkv_prompt_kernel(x_ref, w_ref, ca_ref, sa_ref, sb_ref, k_ref, v_ref, qb_ref, kb_ref, vt_ref, **kw):
    q, k, v = _qkv_rows(x_ref, w_ref, ca_ref, sa_ref, sb_ref, k_ref, v_ref, **kw)
    qb_ref[...] = q.astype(BF16)
    kb_ref[...] = k.astype(BF16)
    vt_ref[...] = v.T.astype(BF16)


def _qkv_decode_kernel(x_ref, w_ref, ca_ref, sa_ref, sb_ref, k_ref, v_ref, q_ref, **kw):
    q, _, _ = _qkv_rows(x_ref, w_ref, ca_ref, sa_ref, sb_ref, k_ref, v_ref, **kw)
    q_ref[...] = q


def _rope_tables(pos):
    half = ROT_DIM // 2
    inv = ROPE_THETA ** (-jnp.arange(0, ROT_DIM, 2, dtype=F32) / ROT_DIM)
    ang = pos.astype(F32)[:, None] * inv
    cos, sin = jnp.cos(ang), jnp.sin(ang)
    n = pos.shape[0]
    pad = jnp.zeros((n, DA_HEAD_DIM - ROT_DIM), F32)
    zero = jnp.zeros((n, half), F32)
    ca = jnp.concatenate([cos, cos, pad + 1.0], axis=1)
    sa = jnp.concatenate([-sin, zero, pad], axis=1)
    sb = jnp.concatenate([zero, sin, pad], axis=1)
    rep = LANES // DA_HEAD_DIM
    return jnp.tile(ca, (1, rep)), jnp.tile(sa, (1, rep)), jnp.tile(sb, (1, rep))


def _qkv(x, w, tables, for_flash):
    m, d = x.shape
    width = w.shape[1] // 3
    n_heads = width // LANES
    tm = _row_tile(m)
    nt = tables[0].shape[0] // tm
    row = lambda i: (i, 0)
    tab = pl.BlockSpec((tm, LANES), lambda i: (i % nt, 0))
    full = pl.BlockSpec((tm, width), row)
    native = pl.BlockSpec((tm * n_heads, LANES), row)
    native_shape = jax.ShapeDtypeStruct((m * n_heads, LANES), F32)
    if for_flash:
        body = _qkv_prompt_kernel
        out_specs = [native, native, full, full, pl.BlockSpec((None, width, tm), lambda i: (i, 0, 0))]
        out_shape = [native_shape, native_shape, jax.ShapeDtypeStruct((m, width), BF16),
                     jax.ShapeDtypeStruct((m, width), BF16), jax.ShapeDtypeStruct((m // tm, width, tm), BF16)]
    else:
        body = _qkv_decode_kernel
        out_specs = [native, native, full]
        out_shape = [native_shape, native_shape, jax.ShapeDtypeStruct((m, width), F32)]
    return pl.pallas_call(
        functools.partial(body, width=width, scale=DA_HEAD_DIM ** -0.5 * math.log2(math.e)),
        grid=(m // tm,),
        in_specs=[pl.BlockSpec((tm, d), row), _resident(w.shape), tab, tab, tab],
        out_specs=out_specs,
        out_shape=out_shape,
        compiler_params=_params("parallel"),
    )(x, w, *tables)


def _finish_kernel(*refs, n_mix, alpha, ff_chunk):
    x_ref = refs[0]
    y_refs = refs[1:1 + n_mix]
    wo_refs = refs[1 + n_mix:1 + 2 * n_mix]
    (p_ref, g1_ref, b1_ref, g2_ref, b2_ref, w1_ref, w2_ref, wp_ref, wg_ref, o_ref) = refs[1 + 2 * n_mix:]
    mix = _dot(y_refs[0][...], wo_refs[0][...])
    for y_ref, wo_ref in zip(y_refs[1:], wo_refs[1:]):
        mix = mix + _dot(y_ref[...], wo_ref[...])
    x1 = _layer_norm(alpha * x_ref[...] + mix, g1_ref[...], b1_ref[...])
    x1b = x1.astype(BF16)
    h = None
    for c in range(0, w1_ref.shape[1], ff_chunk):
        a = jnp.maximum(jnp.dot(x1b, w1_ref[:, c:c + ff_chunk], preferred_element_type=F32), 0.0)
        part = jnp.dot((a * a).astype(BF16), w2_ref[c:c + ff_chunk, :], preferred_element_type=F32)
        h = part if h is None else h + part
    x2 = _layer_norm(alpha * x1 + h, g2_ref[...], b2_ref[...])
    gate = _sigmoid(_dot(x2, wg_ref[...]))
    o_ref[...] = x2 + gate * _dot(p_ref[...], wp_ref[...])


def _finish(x, mixes, w_outs, p, g1, b1, g2, b2, w1, w2, wp, wg, alpha):
    m, d = x.shape
    tm = _row_tile(m)
    row = lambda i: (i, 0)
    vec = lambda a: a.reshape(1, d)
    in_specs = ([pl.BlockSpec((tm, d), row)]
                + [pl.BlockSpec((tm, y.shape[1]), row) for y in mixes]
                + [_resident(w.shape) for w in w_outs]
                + [pl.BlockSpec((tm, p.shape[1]), row)]
                + [_resident((1, d))] * 4
                + [_resident(w.shape) for w in (w1, w2, wp, wg)])
    return pl.pallas_call(
        functools.partial(_finish_kernel, n_mix=len(mixes), alpha=alpha, ff_chunk=1024),
        grid=(m // tm,),
        in_specs=in_specs,
        out_specs=pl.BlockSpec((tm, d), row),
        out_shape=jax.ShapeDtypeStruct((m, d), F32),
        compiler_params=_params("parallel"),
    )(x, *mixes, *w_outs, p, vec(g1), vec(b1), vec(g2), vec(b2), w1, w2, wp, wg)


def _s5_prep_kernel(lr_ref, li_ref, ls_ref, br_ref, bi_ref, ar_ref, ai_ref, bbr_ref, bbi_ref):
    lr, li = lr_ref[...], li_ref[...]
    step = jnp.exp(ls_ref[...])
    mag = jnp.exp(lr * step)
    a_re = mag * jnp.cos(li * step)
    a_im = mag * jnp.sin(li * step)
    den = lr * lr + li * li
    coef_re = ((a_re - 1.0) * lr + a_im * li) / den
    coef_im = (a_im * lr - (a_re - 1.0) * li) / den
    br, bi = br_ref[...], bi_ref[...]
    ar_ref[...] = a_re
    ai_ref[...] = a_im
    bbr_ref[...] = coef_re * br - coef_im * bi
    bbi_ref[...] = coef_re * bi + coef_im * br


def _s5_prep(lam_re, lam_im, log_step, b_re, b_im, c_re, c_im):
    g, p = lam_re.shape
    hh = b_re.shape[-1]
    a_re, a_im, bb_re, bb_im = pl.pallas_call(
        _s5_prep_kernel,
        out_shape=[jax.ShapeDtypeStruct((g, 1, p), F32)] * 2 + [jax.ShapeDtypeStruct((g, hh, p), F32)] * 2,
    )(lam_re.reshape(g, 1, p), lam_im.reshape(g, 1, p), log_step.reshape(g, 1, 1),
      b_re.transpose(0, 2, 1), b_im.transpose(0, 2, 1))
    a = jnp.concatenate([a_re.reshape(1, g * p), a_im.reshape(1, g * p)], axis=1)
    gt = LANES // hh
    nt = g // gt
    eye = jnp.eye(gt, dtype=bool)

    def blockdiag_in(bb):
        bb = bb.reshape(nt, gt, hh, p)
        return jnp.where(eye[None, :, None, :, None], bb[:, :, :, None, :], 0.0).reshape(nt, gt * hh, gt * p)

    def blockdiag_out(c):
        c = c.transpose(0, 2, 1).reshape(nt, gt, p, hh)
        return jnp.where(eye[None, :, None, :, None], c[:, :, :, None, :], 0.0).reshape(nt, gt * p, gt * hh)

    bmat = jnp.concatenate([blockdiag_in(bb_re), blockdiag_in(bb_im)], axis=2).astype(BF16)
    cmat = jnp.concatenate([blockdiag_out(c_re), -blockdiag_out(c_im)], axis=1).astype(BF16)
    return a, bmat, cmat


def _s5_in_map(u, bmat_ref, store):
    for t in range(bmat_ref.shape[0]):
        part = _dot(u[:, t * LANES:(t + 1) * LANES], bmat_ref[t])
        s = part.shape[1] // 2
        store(t * s, part[:, 0:s], part[:, s:2 * s])


def _s5_out_map(load, cmat_ref):
    ys = []
    for t in range(cmat_ref.shape[0]):
        s = cmat_ref.shape[1] // 2
        hr, hi = load(t * s, s)
        ys.append(_dot(hr, cmat_ref[t, 0:s, :]) + _dot(hi, cmat_ref[t, s:2 * s, :]))
    return jnp.concatenate(ys, axis=1)


def _s5_tail(y, u, d, wg, bg):
    y = jax.nn.gelu(y + d * u)
    return y * _sigmoid(_dot(y, wg) + bg)


def _s5_scan_kernel(u_ref, h0_ref, a_ref, bmat_ref, cmat_ref, d_ref, wg_ref, bg_ref,
                    y_ref, hout_ref, col_ref, up_ref, bu_ref, pw_ref, hin_ref, carry_ref):
    n_seg, seg = S5_SEGMENTS, S5_SEG_LEN
    ns = a_ref.shape[1] // 2
    lw = S5_LANE_CHUNK
    first_chunk = pl.program_id(1) == 0

    @pl.when(jnp.logical_and(pl.program_id(0) == 0, first_chunk))
    def _():
        ar, ai = a_ref[:, 0:ns], a_ref[:, ns:2 * ns]
        pw_ref[0:1, :] = a_ref[...]

        def body(k, c):
            pr, pi = c
            nr, ni = pr * ar - pi * ai, pr * ai + pi * ar
            pw_ref[pl.ds(k, 1), 0:ns] = nr
            pw_ref[pl.ds(k, 1), ns:2 * ns] = ni
            return nr, ni

        lax.fori_loop(1, seg, body, (ar, ai))

    @pl.when(first_chunk)
    def _():
        carry_ref[...] = h0_ref[...]

    n_col = u_ref.shape[1] // LANES
    for k in range(n_col):
        col_ref[k] = u_ref[:, k * LANES:(k + 1) * LANES]
    for t in range(seg):
        for k in range(n_col):
            up_ref[t * n_seg:(t + 1) * n_seg, k * LANES:(k + 1) * LANES] = col_ref[k, pl.ds(t, n_seg, stride=seg), :]
    def store_bu(off, re, im):
        bu_ref[:, off:off + re.shape[1]] = re
        bu_ref[:, ns + off:ns + off + im.shape[1]] = im

    _s5_in_map(up_ref[...], bmat_ref, store_bu)

    for lc in range(0, ns, lw):
        ar = jnp.broadcast_to(a_ref[:, lc:lc + lw], (n_seg, lw))
        ai = jnp.broadcast_to(a_ref[:, ns + lc:ns + lc + lw], (n_seg, lw))

        def sweep(t, c, lc=lc, ar=ar, ai=ai):
            hr, hi = c
            r0 = pl.multiple_of(t * n_seg, n_seg)
            nr = ar * hr - ai * hi + bu_ref[pl.ds(r0, n_seg), lc:lc + lw]
            ni = ar * hi + ai * hr + bu_ref[pl.ds(r0, n_seg), ns + lc:ns + lc + lw]
            bu_ref[pl.ds(r0, n_seg), lc:lc + lw] = nr
            bu_ref[pl.ds(r0, n_seg), ns + lc:ns + lc + lw] = ni
            return nr, ni

        zero = jnp.zeros((n_seg, lw), F32)
        lax.fori_loop(0, seg, sweep, (zero, zero), unroll=2)

    asr, asi = pw_ref[seg - 1:seg, 0:ns], pw_ref[seg - 1:seg, ns:2 * ns]
    cr, ci = carry_ref[:, 0:ns], carry_ref[:, ns:2 * ns]
    last = (seg - 1) * n_seg
    for s in range(n_seg):
        hin_ref[s:s + 1, 0:ns] = cr
        hin_ref[s:s + 1, ns:2 * ns] = ci
        er = bu_ref[last + s:last + s + 1, 0:ns]
        ei = bu_ref[last + s:last + s + 1, ns:2 * ns]
        cr, ci = asr * cr - asi * ci + er, asr * ci + asi * cr + ei
    carry_ref[:, 0:ns] = cr
    carry_ref[:, ns:2 * ns] = ci
    hout_ref[...] = carry_ref[...]

    for lc in range(0, ns, lw):
        hr_in = hin_ref[:, lc:lc + lw]
        hi_in = hin_ref[:, ns + lc:ns + lc + lw]

        def fix(t, c, lc=lc, hr_in=hr_in, hi_in=hi_in):
            r0 = pl.multiple_of(t * n_seg, n_seg)
            pr = jnp.broadcast_to(pw_ref[pl.ds(t, 1), lc:lc + lw], (n_seg, lw))
            pi = jnp.broadcast_to(pw_ref[pl.ds(t, 1), ns + lc:ns + lc + lw], (n_seg, lw))
            bu_ref[pl.ds(r0, n_seg), lc:lc + lw] += pr * hr_in - pi * hi_in
            bu_ref[pl.ds(r0, n_seg), ns + lc:ns + lc + lw] += pr * hi_in + pi * hr_in
            return c

        lax.fori_loop(0, seg, fix, 0, unroll=2)

    y = _s5_out_map(lambda off, s: (bu_ref[:, off:off + s], bu_ref[:, ns + off:ns + off + s]), cmat_ref)
    out = _s5_tail(y, up_ref[...], d_ref[...], wg_ref[...], bg_ref[...])
    for k in range(n_col):
        col_ref[k] = out[:, k * LANES:(k + 1) * LANES]
    for r0 in range(0, n_seg * seg, n_seg):
        s, t0 = divmod(r0, seg)
        for k in range(n_col):
            y_ref[r0:r0 + n_seg, k * LANES:(k + 1) * LANES] = col_ref[k, pl.ds(t0 * n_seg + s, n_seg, stride=n_seg), :]


def _s5_scan(u, h0, a, bmat, cmat, d, wg, bg):
    bsz, seq, w = u.shape
    q = S5_SEGMENTS * S5_SEG_LEN
    ns2 = a.shape[1]
    return pl.pallas_call(
        _s5_scan_kernel,
        grid=(bsz, seq // q),
        in_specs=[pl.BlockSpec((None, q, w), lambda b, c: (b, c, 0)),
                  pl.BlockSpec((None, 1, ns2), lambda b, c: (b, 0, 0)),
                  _resident(a.shape), _resident(bmat.shape), _resident(cmat.shape),
                  _resident((1, w)), _resident(wg.shape), _resident((1, w))],
        out_specs=[pl.BlockSpec((None, q, w), lambda b, c: (b, c, 0)),
                   pl.BlockSpec((None, 1, ns2), lambda b, c: (b, 0, 0))],
        out_shape=[jax.ShapeDtypeStruct((bsz, seq, w), F32), jax.ShapeDtypeStruct((bsz, 1, ns2), F32)],
        scratch_shapes=[pltpu.VMEM((w // LANES, q, LANES), F32), pltpu.VMEM((q, w), F32), pltpu.VMEM((q, ns2), F32),
                        pltpu.VMEM((S5_SEG_LEN, ns2), F32), pltpu.VMEM((S5_SEGMENTS, ns2), F32),
                        pltpu.VMEM((1, ns2), F32)],
        compiler_params=_params("arbitrary", "arbitrary"),
    )(u, h0, a, bmat, cmat, d.reshape(1, w), wg, bg.reshape(1, w))


def _s5_step_kernel(u_ref, h0_ref, a_ref, bmat_ref, cmat_ref, d_ref, wg_ref, bg_ref, y_ref, hout_ref):
    ns = a_ref.shape[1] // 2
    u = u_ref[...]

    def advance(off, bu_re, bu_im):
        s = bu_re.shape[1]
        ar, ai = a_ref[:, off:off + s], a_ref[:, ns + off:ns + off + s]
        h0r, h0i = h0_ref[:, off:off + s], h0_ref[:, ns + off:ns + off + s]
        hout_ref[:, off:off + s] = ar * h0r - ai * h0i + bu_re
        hout_ref[:, ns + off:ns + off + s] = ar * h0i + ai * h0r + bu_im

    _s5_in_map(u, bmat_ref, advance)
    y = _s5_out_map(lambda off, s: (hout_ref[:, off:off + s], hout_ref[:, ns + off:ns + off + s]), cmat_ref)
    y_ref[...] = _s5_tail(y, u, d_ref[...], wg_ref[...], bg_ref[...])


def _s5_step(u, h0, a, bmat, cmat, d, wg, bg):
    bsz, w = u.shape
    return pl.pallas_call(
        _s5_step_kernel,
        out_shape=[jax.ShapeDtypeStruct((bsz, w), F32), jax.ShapeDtypeStruct(h0.shape, F32)],
        compiler_params=pltpu.CompilerParams(vmem_limit_bytes=VMEM_LIMIT_BYTES),
    )(u, h0, a, bmat, cmat, d.reshape(1, w), wg, bg.reshape(1, w))


def _head_expander(n_heads):
    rows = lax.broadcasted_iota(jnp.int32, (LANES, n_heads * SSD_HEAD_DIM), 0)
    cols = lax.broadcasted_iota(jnp.int32, (LANES, n_heads * SSD_HEAD_DIM), 1)
    return (cols // SSD_HEAD_DIM == rows).astype(F32)


def _ssd_scan_kernel(z_ref, xbc_ref, dt_ref, cs_ref, h0_ref, cw_ref, cb_ref, dtb_ref, alog_ref, dx_ref, nw_ref,
                     y_ref, cso_ref, ho_ref, xpad_ref, st_ref, *, n_heads):
    q = z_ref.shape[0]
    width = n_heads * SSD_HEAD_DIM
    n_state = (xbc_ref.shape[1] - width) // (2 * SSD_GROUPS)
    kc = cw_ref.shape[0]
    hg = n_heads // SSD_GROUPS
    gw = hg * SSD_HEAD_DIM
    c = pl.program_id(1)
    tail0 = SUBLANES - (kc - 1)

    @pl.when(c == 0)
    def _():
        xpad_ref[tail0:SUBLANES, :] = cs_ref[...]
        st_ref[...] = h0_ref[...].T

    @pl.when(c != 0)
    def _():
        xpad_ref[tail0:SUBLANES, :] = xpad_ref[q + tail0:q + SUBLANES, :]

    xbc = xbc_ref[...]
    xpad_ref[SUBLANES:SUBLANES + q, :] = xbc
    cso_ref[...] = xbc[q - (kc - 1):q, :]

    conv = cb_ref[...] + xbc * cw_ref[kc - 1:kc, :]
    for k in range(kc - 1):
        conv = conv + xpad_ref[tail0 + k:tail0 + k + q, :] * cw_ref[k:k + 1, :]
    conv = _silu(conv)
    xs = conv[:, 0:width]
    bm = [conv[:, width + g * n_state:width + (g + 1) * n_state] for g in range(SSD_GROUPS)]
    cm = [conv[:, width + (SSD_GROUPS + g) * n_state:width + (SSD_GROUPS + g + 1) * n_state]
          for g in range(SSD_GROUPS)]

    dt = jax.nn.softplus(dt_ref[...] + dtb_ref[...])
    av = dt * (-jnp.exp(alog_ref[...]))
    rows = lax.broadcasted_iota(jnp.int32, (q, q), 0)
    cols = lax.broadcasted_iota(jnp.int32, (q, q), 1)
    causal = rows >= cols
    a_cs = _dot_exact(causal.astype(F32), av, 0)
    a_cs_t = a_cs.T
    expand = _head_expander(n_heads)
    a_cs_x = _dot_exact(a_cs, expand, 1)
    tot_x = a_cs_x[q - 1:q, :]
    xdt = xs * _dot_exact(dt, expand, 1)
    cb = [_dot_nt(cm[g], bm[g]) for g in range(SSD_GROUPS)]

    y_parts = []
    for h in range(n_heads):
        g = h // hg
        seg = jnp.where(causal, a_cs[:, h:h + 1] - a_cs_t[h:h + 1, :], -jnp.inf)
        gmat = cb[g] * jnp.exp(seg)
        y_parts.append(_dot(gmat, xdt[:, h * SSD_HEAD_DIM:(h + 1) * SSD_HEAD_DIM]))
    y = jnp.concatenate(y_parts, axis=1)

    st = st_ref[...]
    y_off = jnp.concatenate([_dot(cm[g], st[:, g * gw:(g + 1) * gw]) for g in range(SSD_GROUPS)], axis=1)
    y = y + y_off * jnp.exp(a_cs_x) + dx_ref[...] * xs
    xdec = xdt * jnp.exp(tot_x - a_cs_x)
    upd = jnp.concatenate([_dot(bm[g].T, xdec[:, g * gw:(g + 1) * gw]) for g in range(SSD_GROUPS)], axis=1)
    st_new = st * jnp.exp(tot_x) + upd
    st_ref[...] = st_new

    @pl.when(c == pl.num_programs(1) - 1)
    def _():
        ho_ref[...] = st_new.T

    z = z_ref[...]
    y_ref[...] = _rms_norm(y * _silu(z), nw_ref[...]).astype(y_ref.dtype)


def _pad_lanes(v):
    return jnp.pad(v.reshape(1, -1), ((0, 0), (0, LANES - v.shape[-1])))


def _ssd_scan(z, xbc, dt, conv_state, h0, conv_w, conv_b, dt_bias, a_log, d_skip, norm_w):
    bsz, seq, width = z.shape
    cdim = xbc.shape[2]
    n_heads = dt_bias.shape[0]
    kc = conv_w.shape[0]
    n_state = h0.shape[2]
    q = SSD_CHUNK
    blk = lambda n: pl.BlockSpec((None, q, n), lambda b, c: (b, c, 0))
    per_b = lambda s: pl.BlockSpec((None,) + s, lambda b, c: (b, 0, 0))
    return pl.pallas_call(
        functools.partial(_ssd_scan_kernel, n_heads=n_heads),
        grid=(bsz, seq // q),
        in_specs=[blk(width), blk(cdim), blk(LANES), per_b((kc - 1, cdim)), per_b((width, n_state)),
                  _resident((kc, cdim)), _resident((1, cdim)), _resident((1, LANES)), _resident((1, LANES)),
                  _resident((1, width)), _resident((1, width))],
        out_specs=[blk(width), per_b((kc - 1, cdim)), per_b((width, n_state))],
        out_shape=[jax.ShapeDtypeStruct((bsz, seq, width), BF16),
                   jax.ShapeDtypeStruct((bsz, kc - 1, cdim), F32),
                   jax.ShapeDtypeStruct((bsz, width, n_state), F32)],
        scratch_shapes=[pltpu.VMEM((q + SUBLANES, cdim), F32), pltpu.VMEM((n_state, width), F32)],
        compiler_params=_params("arbitrary", "arbitrary"),
    )(z, xbc, dt, conv_state, h0, conv_w, conv_b.reshape(1, cdim), _pad_lanes(dt_bias), _pad_lanes(a_log),
      jnp.repeat(d_skip, SSD_HEAD_DIM).reshape(1, width), norm_w.reshape(1, width))


def _ssd_pre_kernel(xbc_ref, dt_ref, cs_ref, cw_ref, cb_ref, dtb_ref, alog_ref,
                    conv_ref, cso_ref, dtx_ref, dax_ref, *, n_heads):
    kc = cw_ref.shape[0]
    width = n_heads * SSD_HEAD_DIM
    xbc = xbc_ref[...]
    conv = cb_ref[...] + xbc * cw_ref[kc - 1:kc, :]
    for k in range(kc - 1):
        conv = conv + cs_ref[k] * cw_ref[k:k + 1, :]
    for k in range(kc - 2):
        cso_ref[k] = cs_ref[k + 1]
    cso_ref[kc - 2] = xbc
    conv = _silu(conv)
    conv_ref[...] = conv
    dt = jax.nn.softplus(dt_ref[...] + dtb_ref[...])
    expand = _head_expander(n_heads)
    dtx_ref[...] = conv[:, 0:width] * _dot_exact(dt, expand, 1)
    dax_ref[...] = jnp.exp(_dot_exact(dt * (-jnp.exp(alog_ref[...])), expand, 1))


def _ssd_state_kernel(st_ref, dtxt_ref, daxt_ref, conv_ref, z_ref, dx_ref, nw_ref, y_ref, sto_ref, *, n_heads):
    b = pl.program_id(0)
    width, n_state = st_ref.shape
    gw = width // SSD_GROUPS
    nb = dtxt_ref.shape[1]
    onehot = (lax.broadcasted_iota(jnp.int32, (nb, n_state), 0) == b).astype(F32)
    dtx_col = _dot_exact(dtxt_ref[...], onehot, 1)
    da_col = _dot_exact(daxt_ref[...], onehot, 1)
    row = conv_ref[pl.ds(b, 1), :]
    ys = []
    for g in range(SSD_GROUPS):
        r = slice(g * gw, (g + 1) * gw)
        bm = row[:, width + g * n_state:width + (g + 1) * n_state]
        cm = row[:, width + (SSD_GROUPS + g) * n_state:width + (SSD_GROUPS + g + 1) * n_state]
        new = st_ref[r, :] * da_col[r, :] + dtx_col[r, :] * bm
        sto_ref[r, :] = new
        ys.append(_dot_nt(jnp.broadcast_to(cm, (SUBLANES, n_state)), new)[0:1, :])
    xs = row[:, 0:width]
    y = jnp.concatenate(ys, axis=1) + dx_ref[...] * xs
    y_ref[...] = _rms_norm(y * _silu(z_ref[pl.ds(b, 1), :]), nw_ref[...])


def _ssd_step(z, xbc, dt, conv_state, h0, conv_w, conv_b, dt_bias, a_log, d_skip, norm_w):
    bsz, width = z.shape
    cdim = xbc.shape[1]
    n_heads = dt_bias.shape[0]
    kc = conv_w.shape[0]
    n_state = h0.shape[2]
    conv, new_conv, dtx, dax = pl.pallas_call(
        functools.partial(_ssd_pre_kernel, n_heads=n_heads),
        out_shape=[jax.ShapeDtypeStruct((bsz, cdim), F32), jax.ShapeDtypeStruct((kc - 1, bsz, cdim), F32),
                   jax.ShapeDtypeStruct((bsz, width), F32), jax.ShapeDtypeStruct((bsz, width), F32)],
        compiler_params=pltpu.CompilerParams(vmem_limit_bytes=VMEM_LIMIT_BYTES),
    )(xbc, dt, conv_state.transpose(1, 0, 2), conv_w, conv_b.reshape(1, cdim), _pad_lanes(dt_bias),
      _pad_lanes(a_log))
    pad_t = lambda m: jnp.pad(m, ((0, LANES - bsz), (0, 0))).T
    y, st = pl.pallas_call(
        functools.partial(_ssd_state_kernel, n_heads=n_heads),
        grid=(bsz,),
        in_specs=[pl.BlockSpec((None, width, n_state), lambda b: (b, 0, 0)),
                  _resident((width, LANES)), _resident((width, LANES)), _resident((bsz, cdim)),
                  _resident((bsz, width)), _resident((1, width)), _resident((1, width))],
        out_specs=[pl.BlockSpec((None, 1, width), lambda b: (b, 0, 0)),
                   pl.BlockSpec((None, width, n_state), lambda b: (b, 0, 0))],
        out_shape=[jax.ShapeDtypeStruct((bsz, 1, width), F32), jax.ShapeDtypeStruct(h0.shape, F32)],
        compiler_params=_params("arbitrary"),
    )(h0, pad_t(dtx), pad_t(dax), conv, z, jnp.repeat(d_skip, SSD_HEAD_DIM).reshape(1, width),
      norm_w.reshape(1, width))
    return y.reshape(bsz, width), new_conv.transpose(1, 0, 2), st


def _diff_lambda(lq1_ref, lk1_ref, lq2_ref, lk2_ref, lam_init):
    s1 = jnp.sum(lq1_ref[...] * lk1_ref[...], axis=-1, keepdims=True)
    s2 = jnp.sum(lq2_ref[...] * lk2_ref[...], axis=-1, keepdims=True)
    return jnp.exp(s1) - jnp.exp(s2) + lam_init


def _flash_kernel(q_ref, k_ref, vt_ref, lq1_ref, lk1_ref, lq2_ref, lk2_ref, g_ref, o_ref, *, lam_init):
    tq, tk, dh = ATTN_TQ, ATTN_TK, DA_HEAD_DIM
    qi = pl.program_id(2)
    q = q_ref[...]
    lane = lax.broadcasted_iota(jnp.int32, q.shape, 1)
    zero = jnp.zeros_like(q)
    qs = (jnp.where(lane < dh, q, zero), jnp.where(lane >= dh, q, zero))

    sw = ATTN_Q_STRIPE
    chains = [(c, r) for r in range(tq // sw) for c in range(2)]
    q_parts = [qs[c][r * sw:(r + 1) * sw, :] for c, r in chains]

    def absorb(blocks, carry):
        def block_scores(j, masked):
            kb = k_ref[pl.ds(pl.multiple_of(j * tk, tk), tk), :]
            out = []
            for (c, r), qp in zip(chains, q_parts):
                nk = min(tk, (r + 1) * sw) if masked else tk
                out.append(lax.dot_general(kb[0:nk], qp, (((1,), (1,)), ((), ())), preferred_element_type=F32))
            return out

        scores = [block_scores(*blocks[0])]
        for i, (j, masked) in enumerate(blocks):
            if i + 1 < len(blocks):
                scores.append(block_scores(*blocks[i + 1]))
            vt = vt_ref[j]
            out = []
            for (c, r), s, (m, l, acc) in zip(chains, scores[i], carry):
                nk = s.shape[0]
                if masked:
                    keys = lax.broadcasted_iota(jnp.int32, (nk, sw), 0)
                    queries = lax.broadcasted_iota(jnp.int32, (nk, sw), 1) + r * sw
                    s = jnp.where(keys <= queries, s, NEG_BIG)
                m_new = jnp.maximum(m, jnp.max(s, axis=0, keepdims=True))
                alpha = jnp.exp2(m - m_new)
                p = jnp.exp2(s - m_new)
                l = alpha * l + jnp.sum(p, axis=0, keepdims=True)
                acc = alpha * acc + jnp.dot(vt[:, 0:nk], p.astype(BF16), preferred_element_type=F32)
                out.append((m_new, l, acc))
            carry = tuple(out)
        return carry

    def finish(carry):
        l0, l1 = (jnp.concatenate([carry[i][1] for i, (c, _) in enumerate(chains) if c == cc], axis=1)
                  for cc in range(2))
        a0, a1 = (jnp.concatenate([carry[i][2] for i, (c, _) in enumerate(chains) if c == cc], axis=1)
                  for cc in range(2))
        lam = _diff_lambda(lq1_ref, lk1_ref, lq2_ref, lk2_ref, lam_init)
        o = (a0 * (1.0 / l0) - lam * (a1 * (1.0 / l1))).T
        o_ref[...] = (_rms_norm(o, g_ref[...]) * (1.0 - lam_init)).astype(o_ref.dtype)

    n = ATTN_BLOCKS_PER_TRIP
    init = tuple((jnp.full((1, sw), NEG_BIG, F32), jnp.zeros((1, sw), F32), jnp.zeros((2 * dh, sw), F32))
                 for _ in chains)
    carry = lax.fori_loop(0, qi // n, lambda p, c: absorb([(n * p + i, False) for i in range(n)], c), init)

    for rem in range(n):
        @pl.when(qi % n == rem)
        def _(rem=rem):
            finish(absorb([(qi - rem + i, False) for i in range(rem)] + [(qi, True)], carry))


def _flash(qb, kb, vt, lq1, lk1, lq2, lk2, subln_w, lam_init):
    assert ATTN_TQ == ATTN_TK == ROW_TILE
    bsz, seq, width = qb.shape
    hw = 2 * DA_HEAD_DIM
    nk = seq // ATTN_TK
    vec = lambda a: a.reshape(1, -1)
    kv = pl.BlockSpec((None, seq, hw), lambda b, h, i: (b, 0, h))
    vts = pl.BlockSpec((nk, hw, ATTN_TK), lambda b, h, i: (b, h, 0))
    qo = pl.BlockSpec((None, ATTN_TQ, hw), lambda b, h, i: (b, i, h))
    small = _resident((1, DA_HEAD_DIM))
    return pl.pallas_call(
        functools.partial(_flash_kernel, lam_init=lam_init),
        grid=(bsz, width // hw, seq // ATTN_TQ),
        in_specs=[qo, kv, vts, small, small, small, small, _resident((1, hw))],
        out_specs=qo,
        out_shape=jax.ShapeDtypeStruct((bsz, seq, width), BF16),
        compiler_params=_params("parallel", "parallel", "arbitrary"),
    )(qb, kb, vt, vec(lq1), vec(lk1), vec(lq2), vec(lk2), vec(subln_w))


def _decode_attn_kernel(pt_ref, q_ref, kn_ref, vn_ref, lq1_ref, lk1_ref, lq2_ref, lk2_ref, g_ref, *rest,
                        lam_init, n_pages):
    k_refs = rest[0:n_pages]
    v_refs = rest[n_pages:2 * n_pages]
    o_ref, m_ref, l_ref, acc_ref = rest[2 * n_pages:]
    b, s = pl.program_id(0), pl.program_id(1)
    width = q_ref.shape[1]
    n_sub = width // DA_HEAD_DIM
    hw = 2 * DA_HEAD_DIM
    rows = lax.broadcasted_iota(jnp.int32, (n_sub, width), 0)
    lanes = lax.broadcasted_iota(jnp.int32, (n_sub, width), 1)
    qrow = q_ref[pl.ds(b, 1), :]
    qm = jnp.where(lanes // DA_HEAD_DIM == rows, jnp.broadcast_to(qrow, (n_sub, width)), 0.0).astype(BF16)

    @pl.when(s == 0)
    def _():
        kn = jnp.broadcast_to(kn_ref[pl.ds(b, 1), :], (SUBLANES, width))
        m_ref[...] = _dot_nt(qm, kn)[:, 0:1]
        l_ref[...] = jnp.ones_like(l_ref)
        acc_ref[...] = jnp.broadcast_to(vn_ref[pl.ds(b, 1), :], (n_sub, width))

    n_heads = width // hw
    page = k_refs[0].shape[0] // n_heads
    def by_position(ref):
        return jnp.concatenate([ref[pl.ds(h, page, stride=n_heads), :].astype(BF16) for h in range(n_heads)], axis=1)

    scores = [_dot_nt(qm, by_position(k_ref)) for k_ref in k_refs]
    m_old = m_ref[...]
    m_new = m_old
    for sc in scores:
        m_new = jnp.maximum(m_new, jnp.max(sc, axis=-1, keepdims=True))
    alpha = jnp.exp2(m_old - m_new)
    l = alpha * l_ref[...]
    acc = alpha * acc_ref[...]
    for sc, v_ref in zip(scores, v_refs):
        p = jnp.exp2(sc - m_new)
        l = l + jnp.sum(p, axis=-1, keepdims=True)
        acc = acc + _dot(p, by_position(v_ref))
    acc_ref[...] = acc
    m_ref[...] = m_new
    l_ref[...] = l

    @pl.when(s == pl.num_programs(1) - 1)
    def _():
        lam = _diff_lambda(lq1_ref, lk1_ref, lq2_ref, lk2_ref, lam_init)
        o = acc_ref[...] / l
        own = lanes // hw == rows // 2
        coef = jnp.where(rows % 2 == 0, 1.0, -lam)
        out = jnp.sum(jnp.where(own, coef * o, 0.0), axis=0, keepdims=True)
        g = g_ref[...]
        parts = [_rms_norm(out[:, h * hw:(h + 1) * hw], g) for h in range(width // hw)]
        o_ref[...] = jnp.concatenate(parts, axis=1) * (1.0 - lam_init)


def _decode_attn(page_table, q, k_new, v_new, cache_k, cache_v, layer, lq1, lk1, lq2, lk2, subln_w, lam_init):
    bsz, width = q.shape
    n_tab = page_table.shape[1]
    page_rows = cache_k.shape[2]
    npg = DEC_PAGES_PER_STEP
    assert n_tab % npg == 0
    hw = 2 * DA_HEAD_DIM
    vec = lambda a: a.reshape(1, -1)
    res = lambda shape: pl.BlockSpec(shape, lambda b, s, pt: (0,) * len(shape))

    def page_spec(i):
        return pl.BlockSpec((None, None, page_rows, hw),
                            lambda b, s, pt: (layer, pt[b * n_tab + s * npg + i], 0, 0))

    pages = [page_spec(i) for i in range(npg)]
    n_sub = width // DA_HEAD_DIM
    grid_spec = pltpu.PrefetchScalarGridSpec(
        num_scalar_prefetch=1,
        grid=(bsz, n_tab // npg),
        in_specs=[res((bsz, width))] * 3 + [res((1, DA_HEAD_DIM))] * 4 + [res((1, hw))] + pages + pages,
        out_specs=pl.BlockSpec((None, 1, width), lambda b, s, pt: (b, 0, 0)),
        scratch_shapes=[pltpu.VMEM((n_sub, 1), F32), pltpu.VMEM((n_sub, 1), F32), pltpu.VMEM((n_sub, width), F32)],
    )
    out = pl.pallas_call(
        functools.partial(_decode_attn_kernel, lam_init=lam_init, n_pages=npg),
        grid_spec=grid_spec,
        out_shape=jax.ShapeDtypeStruct((bsz, 1, width), F32),
        compiler_params=_params("arbitrary", "arbitrary"),
    )(page_table.reshape(-1), q, k_new, v_new, vec(lq1), vec(lk1), vec(lq2), vec(lk2), vec(subln_w),
      *([cache_k] * npg), *([cache_v] * npg))
    return out.reshape(bsz, width)


def kernel(x_prompt, x_sample, state_s5_re, state_s5_im, state_ssd, state_conv, cache_k, cache_v, page_table, p_prompt, p_sample, ln1_g, ln1_b, ln2_g, ln2_b, ssm_w_in, s5_lam_re, s5_lam_im, s5_log_step, s5_b_re, s5_b_im, s5_c_re, s5_c_im, s5_d, s5_w_glu, s5_b_glu, ssd_conv_w, ssd_conv_b, ssd_dt_bias, ssd_a_log, ssd_d, ssd_norm_w, ssm_w_out, attn_w_qkv, attn_lq1, attn_lk1, attn_lq2, attn_lk2, attn_subln_w, attn_w_out, mlp_w1, mlp_w2, ple_w_proj, ple_w_gate):
    bp, seq_p, d_model = x_prompt.shape
    bs, seq_s, _ = x_sample.shape
    assert seq_s == 1
    depth = ln1_g.shape[0]
    alpha = (2 * depth) ** 0.25
    s5_w = s5_d.shape[1]
    s5_groups, s5_state = s5_lam_re.shape[1:]
    ns = s5_groups * s5_state
    ssd_heads = ssd_dt_bias.shape[1]
    ssd_w = ssd_heads * SSD_HEAD_DIM
    cdim = ssd_conv_w.shape[2]
    kc = ssd_conv_w.shape[1]
    n_state = state_ssd.shape[-1]
    da_w = attn_w_out.shape[1]
    da_heads = da_w // (2 * DA_HEAD_DIM)
    page = cache_k.shape[2]
    past_len = page_table.shape[1] * page
    tp = bp * seq_p

    xp = x_prompt.reshape(tp, d_model)
    xs = x_sample.reshape(bs, d_model)
    tables_p = _rope_tables(jnp.arange(seq_p))
    tables_s = _rope_tables(jnp.full((bs,), past_len))
    ck = cache_k.reshape(cache_k.shape[0], cache_k.shape[1], page * da_heads, 2 * DA_HEAD_DIM)
    cv = cache_v.reshape(cache_v.shape[0], cache_v.shape[1], page * da_heads, 2 * DA_HEAD_DIM)

    outs = {name: [] for name in ("s5p", "s5s", "ssdp", "ssds", "convp", "convs", "kp", "vp", "ks", "vs")}
    for i in range(depth):
        j = i // 2
        if i % 2 == 0:
            n_main = s5_w + ssd_w + cdim
            w_in = [ssm_w_in[j][:, :n_main].astype(BF16),
                    jnp.pad(ssm_w_in[j][:, n_main:], ((0, 0), (0, LANES - ssd_heads))).astype(BF16)]
            splits = ((s5_w, ssd_w, cdim), (LANES,))
            a, bmat, cmat = _s5_prep(s5_lam_re[j], s5_lam_im[j], s5_log_step[j], s5_b_re[j], s5_b_im[j],
                                     s5_c_re[j], s5_c_im[j])
            wg = s5_w_glu[j].astype(BF16)
            ssd_prm = (ssd_conv_w[j], ssd_conv_b[j], ssd_dt_bias[j], ssd_a_log[j], ssd_d[j], ssd_norm_w[j])

            u, z, xbc, dt = _proj(xp, w_in, splits)
            ya, hfin = _s5_scan(u.reshape(bp, seq_p, s5_w), jnp.zeros((bp, 1, 2 * ns), F32), a, bmat, cmat,
                                s5_d[j], wg, s5_b_glu[j])
            yb, conv_p, st_p = _ssd_scan(z.reshape(bp, seq_p, ssd_w), xbc.reshape(bp, seq_p, cdim),
                                         dt.reshape(bp, seq_p, LANES), jnp.zeros((bp, kc - 1, cdim), F32),
                                         jnp.zeros((bp, ssd_w, n_state), F32), *ssd_prm)
            mixes_p = [ya.reshape(tp, s5_w), yb.reshape(tp, ssd_w)]
            outs["s5p"].append(hfin.reshape(bp, 2, s5_groups, s5_state))
            outs["ssdp"].append(st_p.reshape(bp, ssd_heads, SSD_HEAD_DIM, n_state))
            outs["convp"].append(conv_p)

            u, z, xbc, dt = _proj(xs, w_in, splits)
            h0 = jnp.concatenate([state_s5_re[j].reshape(bs, ns), state_s5_im[j].reshape(bs, ns)], axis=1)
            ya, hnew = _s5_step(u, h0, a, bmat, cmat, s5_d[j], wg, s5_b_glu[j])
            yb, conv_s, st_s = _ssd_step(z, xbc, dt, state_conv[j], state_ssd[j].reshape(bs, ssd_w, n_state),
                                         *ssd_prm)
            mixes_s = [ya, yb]
            outs["s5s"].append(hnew.reshape(bs, 2, s5_groups, s5_state))
            outs["ssds"].append(st_s.reshape(bs, ssd_heads, SSD_HEAD_DIM, n_state))
            outs["convs"].append(conv_s)
            w_out = ssm_w_out[j].astype(BF16)
            w_outs = [w_out[0:s5_w], w_out[s5_w:]]
        else:
            lam_init = 0.8 - 0.6 * math.exp(-0.3 * i)
            w_qkv = attn_w_qkv[j].astype(BF16)
            lam_prm = (attn_lq1[j], attn_lk1[j], attn_lq2[j], attn_lk2[j], attn_subln_w[j])

            k, v, qb, kb, vt = _qkv(xp, w_qkv, tables_p, True)
            shape3 = (bp, seq_p, da_w)
            o = _flash(qb.reshape(shape3), kb.reshape(shape3), vt, *lam_prm, lam_init)
            mixes_p = [o.reshape(tp, da_w)]
            outs["kp"].append(k.reshape(bp, seq_p, da_heads, 2 * DA_HEAD_DIM))
            outs["vp"].append(v.reshape(bp, seq_p, da_heads, 2 * DA_HEAD_DIM))

            k, v, qf = _qkv(xs, w_qkv, tables_s, False)
            o = _decode_attn(page_table, qf, k.reshape(bs, da_w), v.reshape(bs, da_w), ck, cv, j, *lam_prm,
                             lam_init)
            mixes_s = [o]
            outs["ks"].append(k.reshape(bs, 1, da_heads, 2 * DA_HEAD_DIM))
            outs["vs"].append(v.reshape(bs, 1, da_heads, 2 * DA_HEAD_DIM))
            w_outs = [attn_w_out[j].astype(BF16)]

        tail = (ln1_g[i], ln1_b[i], ln2_g[i], ln2_b[i], mlp_w1[i].astype(BF16), mlp_w2[i].astype(BF16),
                ple_w_proj[i].astype(BF16), ple_w_gate[i].astype(BF16), alpha)
        xp = _finish(xp, mixes_p, w_outs, p_prompt[i].reshape(tp, -1), *tail)
        xs = _finish(xs, mixes_s, w_outs, p_sample[i].reshape(bs, -1), *tail)

    s5p, s5s = jnp.stack(outs["s5p"]), jnp.stack(outs["s5s"])
    return (xp.reshape(bp, seq_p, d_model), xs.reshape(bs, 1, d_model),
            s5p[:, :, 0], s5p[:, :, 1], s5s[:, :, 0], s5s[:, :, 1],
            jnp.stack(outs["ssdp"]), jnp.stack(outs["ssds"]), jnp.stack(outs["convp"]), jnp.stack(outs["convs"]),
            jnp.stack(outs["kp"]), jnp.stack(outs["vp"]), jnp.stack(outs["ks"]), jnp.stack(outs["vs"]))
```

```python
import functools
import math

import jax
import jax.numpy as jnp
from jax import lax
from jax.experimental import pallas as pl
from jax.experimental.pallas import tpu as pltpu

F32 = jnp.float32
BF16 = jnp.bfloat16

S5_GROUP = 16
SSD_HEAD_DIM = 64
SSD_GROUPS = 2
DA_HEAD_DIM = 64
ROT_DIM = DA_HEAD_DIM // 4
ROPE_THETA = 500000.0
LN_EPS = 1e-5
RMS_EPS = 1e-5

LANES = 128
SUBLANES = 8
VMEM_LIMIT_BYTES = 56 * 1024 * 1024

ROW_TILE = 512
S5_SEGMENTS = SUBLANES
S5_SEG_LEN = 64
S5_LANE_CHUNK = 512
SSD_CHUNK = 256
SSD_STEP_BATCH = 4
ATTN_TQ = 512
ATTN_TK = 512
ATTN_Q_STRIPE = 256
ATTN_BLOCKS_PER_TRIP = 4
ATTN_ONES_ROWS = 16
DEC_PAGES_PER_STEP = 16
NEG_BIG = -1e30


def _resident(shape):
    zeros = (0,) * len(shape)
    return pl.BlockSpec(shape, lambda *_: zeros, pipeline_mode=pl.Buffered(1))


def _params(*sem):
    return pltpu.CompilerParams(dimension_semantics=sem, vmem_limit_bytes=VMEM_LIMIT_BYTES)


def _dot(a, b):
    return jnp.dot(a.astype(BF16), b.astype(BF16), preferred_element_type=F32)


def _dot_nt(a, b):
    return lax.dot_general(a.astype(BF16), b.astype(BF16), (((1,), (1,)), ((), ())),
                           preferred_element_type=F32)


def _split3(x):
    hi = x.astype(BF16)
    r = x - hi.astype(F32)
    mid = r.astype(BF16)
    lo = (r - mid.astype(F32)).astype(BF16)
    return hi, mid, lo


def _dot_exact(a, b, selector):
    sel = (a, b)[selector].astype(BF16)
    pieces = _split3((b, a)[selector])
    dot = (lambda p: jnp.dot(sel, p, preferred_element_type=F32)) if selector == 0 else \
          (lambda p: jnp.dot(p, sel, preferred_element_type=F32))
    return dot(pieces[0]) + dot(pieces[1]) + dot(pieces[2])


def _sigmoid(x):
    return 0.5 * jnp.tanh(0.5 * x) + 0.5


def _silu(x):
    return x * _sigmoid(x)


def _layer_norm(x, g, b):
    mu = jnp.mean(x, -1, keepdims=True)
    xc = x - mu
    var = jnp.mean(xc * xc, -1, keepdims=True)
    return xc * lax.rsqrt(var + LN_EPS) * g + b


def _rms_norm(x, g):
    return x * lax.rsqrt(jnp.mean(x * x, -1, keepdims=True) + RMS_EPS) * g


def _row_tile(m):
    return ROW_TILE if m % ROW_TILE == 0 else m


def _proj_kernel(x_ref, *refs, splits):
    w_refs, o_refs = refs[:len(splits)], refs[len(splits):]
    xb = x_ref[...].astype(BF16)
    outs = iter(o_refs)
    for w_ref, widths in zip(w_refs, splits):
        off = 0
        for n in widths:
            next(outs)[...] = jnp.dot(xb, w_ref[:, off:off + n], preferred_element_type=F32)
            off += n


def _proj(x, ws, splits):
    m, k = x.shape
    tm = _row_tile(m)
    widths = [n for group in splits for n in group]
    return pl.pallas_call(
        functools.partial(_proj_kernel, splits=splits),
        grid=(m // tm,),
        in_specs=[pl.BlockSpec((tm, k), lambda i: (i, 0))] + [_resident(w.shape) for w in ws],
        out_specs=[pl.BlockSpec((tm, n), lambda i: (i, 0)) for n in widths],
        out_shape=[jax.ShapeDtypeStruct((m, n), F32) for n in widths],
        compiler_params=_params("parallel"),
    )(x, *ws)


def _rope(t, ca, sa, sb):
    half = ROT_DIM // 2
    outs = []
    for h in range(t.shape[1] // LANES):
        c = t[:, h * LANES:(h + 1) * LANES]
        up = pltpu.roll(c, LANES - half, 1)
        dn = pltpu.roll(c, half, 1)
        outs.append(c * ca + up * sa + dn * sb)
    return jnp.concatenate(outs, axis=1)


def _qkv_rows(x_ref, w_ref, ca_ref, sa_ref, sb_ref, k_ref, v_ref, *, width, scale):
    xb = x_ref[...].astype(BF16)
    ca, sa, sb = ca_ref[...], sa_ref[...], sb_ref[...]
    q = _rope(jnp.dot(xb, w_ref[:, 0:width], preferred_element_type=F32), ca, sa, sb) * scale
    k = _rope(jnp.dot(xb, w_ref[:, width:2 * width], preferred_element_type=F32), ca, sa, sb)
    v = jnp.dot(xb, w_ref[:, 2 * width:3 * width], preferred_element_type=F32)
    tm = xb.shape[0]
    n_heads = width // LANES
    for h in range(n_heads):
        k_ref[pl.ds(h, tm, stride=n_heads), :] = k[:, h * LANES:(h + 1) * LANES]
        v_ref[pl.ds(h, tm, stride=n_heads), :] = v[:, h * LANES:(h + 1) * LANES]
    return q, k, v


def _qkv_prompt_kernel(x_ref, w_ref, ca_ref, sa_ref, sb_ref, k_ref, v_ref, qb_ref, kb_ref, vt_ref, **kw):
    q, k, v = _qkv_rows(x_ref, w_ref, ca_ref, sa_ref, sb_ref, k_ref, v_ref, **kw)
    qb_ref[...] = q.astype(BF16)
    kb_ref[...] = k.astype(BF16)
    vt_ref[...] = v.T.astype(BF16)


def _qkv_decode_kernel(x_ref, w_ref, ca_ref, sa_ref, sb_ref, k_ref, v_ref, q_ref, **kw):
    q, _, _ = _qkv_rows(x_ref, w_ref, ca_ref, sa_ref, sb_ref, k_ref, v_ref, **kw)
    q_ref[...] = q


def _rope_tables(pos):
    half = ROT_DIM // 2
    inv = ROPE_THETA ** (-jnp.arange(0, ROT_DIM, 2, dtype=F32) / ROT_DIM)
    ang = pos.astype(F32)[:, None] * inv
    cos, sin = jnp.cos(ang), jnp.sin(ang)
    n = pos.shape[0]
    pad = jnp.zeros((n, DA_HEAD_DIM - ROT_DIM), F32)
    zero = jnp.zeros((n, half), F32)
    ca = jnp.concatenate([cos, cos, pad + 1.0], axis=1)
    sa = jnp.concatenate([-sin, zero, pad], axis=1)
    sb = jnp.concatenate([zero, sin, pad], axis=1)
    rep = LANES // DA_HEAD_DIM
    return jnp.tile(ca, (1, rep)), jnp.tile(sa, (1, rep)), jnp.tile(sb, (1, rep))


def _qkv(x, w, tables, for_flash):
    m, d = x.shape
    width = w.shape[1] // 3
    n_heads = width // LANES
    tm = _row_tile(m)
    nt = tables[0].shape[0] // tm
    row = lambda i: (i, 0)
    tab = pl.BlockSpec((tm, LANES), lambda i: (i % nt, 0))
    full = pl.BlockSpec((tm, width), row)
    native = pl.BlockSpec((tm * n_heads, LANES), row)
    native_shape = jax.ShapeDtypeStruct((m * n_heads, LANES), F32)
    if for_flash:
        body = _qkv_prompt_kernel
        out_specs = [native, native, full, full, pl.BlockSpec((None, width, tm), lambda i: (i, 0, 0))]
        out_shape = [native_shape, native_shape, jax.ShapeDtypeStruct((m, width), BF16),
                     jax.ShapeDtypeStruct((m, width), BF16), jax.ShapeDtypeStruct((m // tm, width, tm), BF16)]
    else:
        body = _qkv_decode_kernel
        out_specs = [native, native, full]
        out_shape = [native_shape, native_shape, jax.ShapeDtypeStruct((m, width), F32)]
    return pl.pallas_call(
        functools.partial(body, width=width, scale=DA_HEAD_DIM ** -0.5 * math.log2(math.e)),
        grid=(m // tm,),
        in_specs=[pl.BlockSpec((tm, d), row), _resident(w.shape), tab, tab, tab],
        out_specs=out_specs,
        out_shape=out_shape,
        compiler_params=_params("parallel"),
    )(x, w, *tables)


def _finish_kernel(*refs, n_mix, alpha, ff_chunk):
    x_ref = refs[0]
    y_refs = refs[1:1 + n_mix]
    wo_refs = refs[1 + n_mix:1 + 2 * n_mix]
    (p_ref, g1_ref, b1_ref, g2_ref, b2_ref, w1_ref, w2_ref, wp_ref, wg_ref, o_ref) = refs[1 + 2 * n_mix:]
    mix = _dot(y_refs[0][...], wo_refs[0][...])
    for y_ref, wo_ref in zip(y_refs[1:], wo_refs[1:]):
        mix = mix + _dot(y_ref[...], wo_ref[...])
    x1 = _layer_norm(alpha * x_ref[...] + mix, g1_ref[...], b1_ref[...])
    x1b = x1.astype(BF16)
    h = None
    for c in range(0, w1_ref.shape[1], ff_chunk):
        a = jnp.maximum(jnp.dot(x1b, w1_ref[:, c:c + ff_chunk], preferred_element_type=F32), 0.0)
        part = jnp.dot((a * a).astype(BF16), w2_ref[c:c + ff_chunk, :], preferred_element_type=F32)
        h = part if h is None else h + part
    x2 = _layer_norm(alpha * x1 + h, g2_ref[...], b2_ref[...])
    gate = _sigmoid(_dot(x2, wg_ref[...]))
    o_ref[...] = x2 + gate * _dot(p_ref[...], wp_ref[...])


def _finish(x, mixes, w_outs, p, g1, b1, g2, b2, w1, w2, wp, wg, alpha):
    m, d = x.shape
    tm = _row_tile(m)
    row = lambda i: (i, 0)
    vec = lambda a: a.reshape(1, d)
    in_specs = ([pl.BlockSpec((tm, d), row)]
                + [pl.BlockSpec((tm, y.shape[1]), row) for y in mixes]
                + [_resident(w.shape) for w in w_outs]
                + [pl.BlockSpec((tm, p.shape[1]), row)]
                + [_resident((1, d))] * 4
                + [_resident(w.shape) for w in (w1, w2, wp, wg)])
    return pl.pallas_call(
        functools.partial(_finish_kernel, n_mix=len(mixes), alpha=alpha, ff_chunk=1024),
        grid=(m // tm,),
        in_specs=in_specs,
        out_specs=pl.BlockSpec((tm, d), row),
        out_shape=jax.ShapeDtypeStruct((m, d), F32),
        compiler_params=_params("parallel"),
    )(x, *mixes, *w_outs, p, vec(g1), vec(b1), vec(g2), vec(b2), w1, w2, wp, wg)


def _s5_prep_kernel(lr_ref, li_ref, ls_ref, br_ref, bi_ref, ar_ref, ai_ref, bbr_ref, bbi_ref):
    lr, li = lr_ref[...], li_ref[...]
    step = jnp.exp(ls_ref[...])
    mag = jnp.exp(lr * step)
    a_re = mag * jnp.cos(li * step)
    a_im = mag * jnp.sin(li * step)
    den = lr * lr + li * li
    coef_re = ((a_re - 1.0) * lr + a_im * li) / den
    coef_im = (a_im * lr - (a_re - 1.0) * li) / den
    br, bi = br_ref[...], bi_ref[...]
    ar_ref[...] = a_re
    ai_ref[...] = a_im
    bbr_ref[...] = coef_re * br - coef_im * bi
    bbi_ref[...] = coef_re * bi + coef_im * br


def _s5_prep(lam_re, lam_im, log_step, b_re, b_im, c_re, c_im):
    g, p = lam_re.shape
    hh = b_re.shape[-1]
    a_re, a_im, bb_re, bb_im = pl.pallas_call(
        _s5_prep_kernel,
        out_shape=[jax.ShapeDtypeStruct((g, 1, p), F32)] * 2 + [jax.ShapeDtypeStruct((g, hh, p), F32)] * 2,
    )(lam_re.reshape(g, 1, p), lam_im.reshape(g, 1, p), log_step.reshape(g, 1, 1),
      b_re.transpose(0, 2, 1), b_im.transpose(0, 2, 1))
    a = jnp.concatenate([a_re.reshape(1, g * p), a_im.reshape(1, g * p)], axis=1)
    gt = LANES // hh
    nt = g // gt
    eye = jnp.eye(gt, dtype=bool)

    def blockdiag_in(bb):
        bb = bb.reshape(nt, gt, hh, p)
        return jnp.where(eye[None, :, None, :, None], bb[:, :, :, None, :], 0.0).reshape(nt, gt * hh, gt * p)

    def blockdiag_out(c):
        c = c.transpose(0, 2, 1).reshape(nt, gt, p, hh)
        return jnp.where(eye[None, :, None, :, None], c[:, :, :, None, :], 0.0).reshape(nt, gt * p, gt * hh)

    bmat = jnp.concatenate([blockdiag_in(bb_re), blockdiag_in(bb_im)], axis=2).astype(BF16)
    cmat = jnp.concatenate([blockdiag_out(c_re), -blockdiag_out(c_im)], axis=1).astype(BF16)
    return a, bmat, cmat


def _s5_in_map(u, bmat_ref, store):
    for t in range(bmat_ref.shape[0]):
        part = _dot(u[:, t * LANES:(t + 1) * LANES], bmat_ref[t])
        s = part.shape[1] // 2
        store(t * s, part[:, 0:s], part[:, s:2 * s])


def _s5_out_map(load, cmat_ref):
    ys = []
    for t in range(cmat_ref.shape[0]):
        s = cmat_ref.shape[1] // 2
        hr, hi = load(t * s, s)
        ys.append(_dot(hr, cmat_ref[t, 0:s, :]) + _dot(hi, cmat_ref[t, s:2 * s, :]))
    return jnp.concatenate(ys, axis=1)


def _s5_tail(y, u, d, wg, bg):
    y = jax.nn.gelu(y + d * u)
    return y * _sigmoid(_dot(y, wg) + bg)


def _s5_scan_kernel(u_ref, h0_ref, a_ref, bmat_ref, cmat_ref, d_ref, wg_ref, bg_ref,
                    y_ref, hout_ref, col_ref, up_ref, bu_ref, pw_ref, hin_ref, carry_ref):
    n_seg, seg = S5_SEGMENTS, S5_SEG_LEN
    ns = a_ref.shape[1] // 2
    lw = S5_LANE_CHUNK
    first_chunk = pl.program_id(1) == 0

    @pl.when(jnp.logical_and(pl.program_id(0) == 0, first_chunk))
    def _():
        ar, ai = a_ref[:, 0:ns], a_ref[:, ns:2 * ns]
        pw_ref[0:n_seg, :] = jnp.broadcast_to(a_ref[...], (n_seg, 2 * ns))

        def body(k, c):
            pr, pi = c
            nr, ni = pr * ar - pi * ai, pr * ai + pi * ar
            r0 = pl.multiple_of(k * n_seg, n_seg)
            pw_ref[pl.ds(r0, n_seg), 0:ns] = jnp.broadcast_to(nr, (n_seg, ns))
            pw_ref[pl.ds(r0, n_seg), ns:2 * ns] = jnp.broadcast_to(ni, (n_seg, ns))
            return nr, ni

        lax.fori_loop(1, seg, body, (ar, ai))

    @pl.when(first_chunk)
    def _():
        carry_ref[...] = h0_ref[...]

    n_col = u_ref.shape[1] // LANES
    for k in range(n_col):
        col_ref[k] = u_ref[:, k * LANES:(k + 1) * LANES]
    for t in range(seg):
        for k in range(n_col):
            up_ref[t * n_seg:(t + 1) * n_seg, k * LANES:(k + 1) * LANES] = col_ref[k, pl.ds(t, n_seg, stride=seg), :]
    def store_bu(off, re, im):
        bu_ref[:, off:off + re.shape[1]] = re
        bu_ref[:, ns + off:ns + off + im.shape[1]] = im

    _s5_in_map(up_ref[...], bmat_ref, store_bu)

    for lc in range(0, ns, lw):
        ar = jnp.broadcast_to(a_ref[:, lc:lc + lw], (n_seg, lw))
        ai = jnp.broadcast_to(a_ref[:, ns + lc:ns + lc + lw], (n_seg, lw))

        def sweep(t, c, lc=lc, ar=ar, ai=ai):
            hr, hi = c
            r0 = pl.multiple_of(t * n_seg, n_seg)
            nr = ar * hr - ai * hi + bu_ref[pl.ds(r0, n_seg), lc:lc + lw]
            ni = ar * hi + ai * hr + bu_ref[pl.ds(r0, n_seg), ns + lc:ns + lc + lw]
            bu_ref[pl.ds(r0, n_seg), lc:lc + lw] = nr
            bu_ref[pl.ds(r0, n_seg), ns + lc:ns + lc + lw] = ni
            return nr, ni

        zero = jnp.zeros((n_seg, lw), F32)
        lax.fori_loop(0, seg, sweep, (zero, zero), unroll=2)

    top = (seg - 1) * n_seg
    asr, asi = pw_ref[top:top + 1, 0:ns], pw_ref[top:top + 1, ns:2 * ns]
    cr, ci = carry_ref[:, 0:ns], carry_ref[:, ns:2 * ns]
    last = (seg - 1) * n_seg
    for s in range(n_seg):
        hin_ref[s:s + 1, 0:ns] = cr
        hin_ref[s:s + 1, ns:2 * ns] = ci
        er = bu_ref[last + s:last + s + 1, 0:ns]
        ei = bu_ref[last + s:last + s + 1, ns:2 * ns]
        cr, ci = asr * cr - asi * ci + er, asr * ci + asi * cr + ei
    carry_ref[:, 0:ns] = cr
    carry_ref[:, ns:2 * ns] = ci
    hout_ref[...] = carry_ref[...]

    for lc in range(0, ns, lw):
        hr_in = hin_ref[:, lc:lc + lw]
        hi_in = hin_ref[:, ns + lc:ns + lc + lw]

        def fix(t, c, lc=lc, hr_in=hr_in, hi_in=hi_in):
            r0 = pl.multiple_of(t * n_seg, n_seg)
            pr = pw_ref[pl.ds(r0, n_seg), lc:lc + lw]
            pi = pw_ref[pl.ds(r0, n_seg), ns + lc:ns + lc + lw]
            bu_ref[pl.ds(r0, n_seg), lc:lc + lw] += pr * hr_in - pi * hi_in
            bu_ref[pl.ds(r0, n_seg), ns + lc:ns + lc + lw] += pr * hi_in + pi * hr_in
            return c

        lax.fori_loop(0, seg, fix, 0, unroll=2)

    y = _s5_out_map(lambda off, s: (bu_ref[:, off:off + s], bu_ref[:, ns + off:ns + off + s]), cmat_ref)
    out = _s5_tail(y, up_ref[...], d_ref[...], wg_ref[...], bg_ref[...])
    for k in range(n_col):
        col_ref[k] = out[:, k * LANES:(k + 1) * LANES]
    for r0 in range(0, n_seg * seg, n_seg):
        s, t0 = divmod(r0, seg)
        for k in range(n_col):
            y_ref[r0:r0 + n_seg, k * LANES:(k + 1) * LANES] = col_ref[k, pl.ds(t0 * n_seg + s, n_seg, stride=n_seg), :]


def _s5_scan(u, h0, a, bmat, cmat, d, wg, bg):
    bsz, seq, w = u.shape
    q = S5_SEGMENTS * S5_SEG_LEN
    ns2 = a.shape[1]
    return pl.pallas_call(
        _s5_scan_kernel,
        grid=(bsz, seq // q),
        in_specs=[pl.BlockSpec((None, q, w), lambda b, c: (b, c, 0)),
                  pl.BlockSpec((None, 1, ns2), lambda b, c: (b, 0, 0)),
                  _resident(a.shape), _resident(bmat.shape), _resident(cmat.shape),
                  _resident((1, w)), _resident(wg.shape), _resident((1, w))],
        out_specs=[pl.BlockSpec((None, q, w), lambda b, c: (b, c, 0)),
                   pl.BlockSpec((None, 1, ns2), lambda b, c: (b, 0, 0))],
        out_shape=[jax.ShapeDtypeStruct((bsz, seq, w), F32), jax.ShapeDtypeStruct((bsz, 1, ns2), F32)],
        scratch_shapes=[pltpu.VMEM((w // LANES, q, LANES), F32), pltpu.VMEM((q, w), F32), pltpu.VMEM((q, ns2), F32),
                        pltpu.VMEM((q, ns2), F32), pltpu.VMEM((S5_SEGMENTS, ns2), F32),
                        pltpu.VMEM((1, ns2), F32)],
        compiler_params=_params("arbitrary", "arbitrary"),
    )(u, h0, a, bmat, cmat, d.reshape(1, w), wg, bg.reshape(1, w))


def _s5_step_kernel(u_ref, h0_ref, a_ref, bmat_ref, cmat_ref, d_ref, wg_ref, bg_ref, y_ref, hout_ref):
    ns = a_ref.shape[1] // 2
    u = u_ref[...]

    def advance(off, bu_re, bu_im):
        s = bu_re.shape[1]
        ar, ai = a_ref[:, off:off + s], a_ref[:, ns + off:ns + off + s]
        h0r, h0i = h0_ref[:, off:off + s], h0_ref[:, ns + off:ns + off + s]
        hout_ref[:, off:off + s] = ar * h0r - ai * h0i + bu_re
        hout_ref[:, ns + off:ns + off + s] = ar * h0i + ai * h0r + bu_im

    _s5_in_map(u, bmat_ref, advance)
    y = _s5_out_map(lambda off, s: (hout_ref[:, off:off + s], hout_ref[:, ns + off:ns + off + s]), cmat_ref)
    y_ref[...] = _s5_tail(y, u, d_ref[...], wg_ref[...], bg_ref[...])


def _s5_step(u, h0, a, bmat, cmat, d, wg, bg):
    bsz, w = u.shape
    return pl.pallas_call(
        _s5_step_kernel,
        out_shape=[jax.ShapeDtypeStruct((bsz, w), F32), jax.ShapeDtypeStruct(h0.shape, F32)],
        compiler_params=pltpu.CompilerParams(vmem_limit_bytes=VMEM_LIMIT_BYTES),
    )(u, h0, a, bmat, cmat, d.reshape(1, w), wg, bg.reshape(1, w))


def _head_expander(n_heads):
    rows = lax.broadcasted_iota(jnp.int32, (LANES, n_heads * SSD_HEAD_DIM), 0)
    cols = lax.broadcasted_iota(jnp.int32, (LANES, n_heads * SSD_HEAD_DIM), 1)
    return (cols // SSD_HEAD_DIM == rows).astype(F32)


def _ssd_scan_kernel(z_ref, xbc_ref, dt_ref, cs_ref, h0_ref, cw_ref, cb_ref, dtb_ref, alog_ref, dx_ref, nw_ref,
                     y_ref, cso_ref, ho_ref, xpad_ref, st_ref, *, n_heads):
    q = z_ref.shape[0]
    width = n_heads * SSD_HEAD_DIM
    n_state = (xbc_ref.shape[1] - width) // (2 * SSD_GROUPS)
    kc = cw_ref.shape[0]
    hg = n_heads // SSD_GROUPS
    gw = hg * SSD_HEAD_DIM
    c = pl.program_id(1)
    tail0 = SUBLANES - (kc - 1)

    @pl.when(c == 0)
    def _():
        xpad_ref[tail0:SUBLANES, :] = cs_ref[...]
        st_ref[...] = h0_ref[...].T

    @pl.when(c != 0)
    def _():
        xpad_ref[tail0:SUBLANES, :] = xpad_ref[q + tail0:q + SUBLANES, :]

    xbc = xbc_ref[...]
    xpad_ref[SUBLANES:SUBLANES + q, :] = xbc
    cso_ref[...] = xbc[q - (kc - 1):q, :]

    conv = cb_ref[...] + xbc * cw_ref[kc - 1:kc, :]
    for k in range(kc - 1):
        conv = conv + xpad_ref[tail0 + k:tail0 + k + q, :] * cw_ref[k:k + 1, :]
    conv = _silu(conv)
    xs = conv[:, 0:width]
    bm = [conv[:, width + g * n_state:width + (g + 1) * n_state] for g in range(SSD_GROUPS)]
    cm = [conv[:, width + (SSD_GROUPS + g) * n_state:width + (SSD_GROUPS + g + 1) * n_state]
          for g in range(SSD_GROUPS)]

    dt = jax.nn.softplus(dt_ref[...] + dtb_ref[...])
    av = dt * (-jnp.exp(alog_ref[...]))
    rows = lax.broadcasted_iota(jnp.int32, (q, q), 0)
    cols = lax.broadcasted_iota(jnp.int32, (q, q), 1)
    causal = rows >= cols
    a_cs = _dot_exact(causal.astype(F32), av, 0)
    a_cs_t = a_cs.T
    expand = _head_expander(n_heads)
    a_cs_x = _dot_exact(a_cs, expand, 1)
    tot_x = a_cs_x[q - 1:q, :]
    xdt = xs * _dot_exact(dt, expand, 1)
    cb = [_dot_nt(cm[g], bm[g]) for g in range(SSD_GROUPS)]

    y_parts = []
    for h in range(n_heads):
        g = h // hg
        seg = jnp.where(causal, a_cs[:, h:h + 1] - a_cs_t[h:h + 1, :], -jnp.inf)
        gmat = cb[g] * jnp.exp(seg)
        y_parts.append(_dot(gmat, xdt[:, h * SSD_HEAD_DIM:(h + 1) * SSD_HEAD_DIM]))
    y = jnp.concatenate(y_parts, axis=1)

    st = st_ref[...]
    y_off = jnp.concatenate([_dot(cm[g], st[:, g * gw:(g + 1) * gw]) for g in range(SSD_GROUPS)], axis=1)
    y = y + y_off * jnp.exp(a_cs_x) + dx_ref[...] * xs
    xdec = xdt * jnp.exp(tot_x - a_cs_x)
    upd = jnp.concatenate([_dot(bm[g].T, xdec[:, g * gw:(g + 1) * gw]) for g in range(SSD_GROUPS)], axis=1)
    st_new = st * jnp.exp(tot_x) + upd
    st_ref[...] = st_new

    @pl.when(c == pl.num_programs(1) - 1)
    def _():
        ho_ref[...] = st_new.T

    z = z_ref[...]
    y_ref[...] = _rms_norm(y * _silu(z), nw_ref[...]).astype(y_ref.dtype)


def _pad_lanes(v):
    return jnp.pad(v.reshape(1, -1), ((0, 0), (0, LANES - v.shape[-1])))


def _ssd_scan(z, xbc, dt, conv_state, h0, conv_w, conv_b, dt_bias, a_log, d_skip, norm_w):
    bsz, seq, width = z.shape
    cdim = xbc.shape[2]
    n_heads = dt_bias.shape[0]
    kc = conv_w.shape[0]
    n_state = h0.shape[2]
    q = SSD_CHUNK
    blk = lambda n: pl.BlockSpec((None, q, n), lambda b, c: (b, c, 0))
    per_b = lambda s: pl.BlockSpec((None,) + s, lambda b, c: (b, 0, 0))
    return pl.pallas_call(
        functools.partial(_ssd_scan_kernel, n_heads=n_heads),
        grid=(bsz, seq // q),
        in_specs=[blk(width), blk(cdim), blk(LANES), per_b((kc - 1, cdim)), per_b((width, n_state)),
                  _resident((kc, cdim)), _resident((1, cdim)), _resident((1, LANES)), _resident((1, LANES)),
                  _resident((1, width)), _resident((1, width))],
        out_specs=[blk(width), per_b((kc - 1, cdim)), per_b((width, n_state))],
        out_shape=[jax.ShapeDtypeStruct((bsz, seq, width), BF16),
                   jax.ShapeDtypeStruct((bsz, kc - 1, cdim), F32),
                   jax.ShapeDtypeStruct((bsz, width, n_state), F32)],
        scratch_shapes=[pltpu.VMEM((q + SUBLANES, cdim), F32), pltpu.VMEM((n_state, width), F32)],
        compiler_params=_params("arbitrary", "arbitrary"),
    )(z, xbc, dt, conv_state, h0, conv_w, conv_b.reshape(1, cdim), _pad_lanes(dt_bias), _pad_lanes(a_log),
      jnp.repeat(d_skip, SSD_HEAD_DIM).reshape(1, width), norm_w.reshape(1, width))


def _ssd_pre_kernel(xbc_ref, dt_ref, cs_ref, cw_ref, cb_ref, dtb_ref, alog_ref,
                    conv_ref, cso_ref, dtx_ref, dax_ref, *, n_heads):
    kc = cw_ref.shape[0]
    width = n_heads * SSD_HEAD_DIM
    xbc = xbc_ref[...]
    conv = cb_ref[...] + xbc * cw_ref[kc - 1:kc, :]
    for k in range(kc - 1):
        conv = conv + cs_ref[k] * cw_ref[k:k + 1, :]
    for k in range(kc - 2):
        cso_ref[k] = cs_ref[k + 1]
    cso_ref[kc - 2] = xbc
    conv = _silu(conv)
    conv_ref[...] = conv
    dt = jax.nn.softplus(dt_ref[...] + dtb_ref[...])
    expand = _head_expander(n_heads)
    dtx_ref[...] = conv[:, 0:width] * _dot_exact(dt, expand, 1)
    dax_ref[...] = jnp.exp(_dot_exact(dt * (-jnp.exp(alog_ref[...])), expand, 1))


def _ssd_state_kernel(st_ref, dtxt_ref, daxt_ref, conv_ref, z_ref, dx_ref, nw_ref, y_ref, sto_ref, *, n_heads):
    per_step, width, n_state = st_ref.shape
    gw = width // SSD_GROUPS
    nb = dtxt_ref.shape[1]
    for i in range(per_step):
        b = pl.program_id(0) * per_step + i
        onehot = (lax.broadcasted_iota(jnp.int32, (nb, n_state), 0) == b).astype(F32)
        dtx_col = _dot_exact(dtxt_ref[...], onehot, 1)
        da_col = _dot_exact(daxt_ref[...], onehot, 1)
        row = conv_ref[pl.ds(b, 1), :]
        ys = []
        for g in range(SSD_GROUPS):
            r = slice(g * gw, (g + 1) * gw)
            bm = row[:, width + g * n_state:width + (g + 1) * n_state]
            cm = row[:, width + (SSD_GROUPS + g) * n_state:width + (SSD_GROUPS + g + 1) * n_state]
            new = st_ref[i, r, :] * da_col[r, :] + dtx_col[r, :] * bm
            sto_ref[i, r, :] = new
            ys.append(_dot_nt(jnp.broadcast_to(cm, (SUBLANES, n_state)), new)[0:1, :])
        xs = row[:, 0:width]
        y = jnp.concatenate(ys, axis=1) + dx_ref[...] * xs
        y_ref[i] = _rms_norm(y * _silu(z_ref[pl.ds(b, 1), :]), nw_ref[...])


def _ssd_step(z, xbc, dt, conv_state, h0, conv_w, conv_b, dt_bias, a_log, d_skip, norm_w):
    bsz, width = z.shape
    cdim = xbc.shape[1]
    n_heads = dt_bias.shape[0]
    kc = conv_w.shape[0]
    n_state = h0.shape[2]
    conv, new_conv, dtx, dax = pl.pallas_call(
        functools.partial(_ssd_pre_kernel, n_heads=n_heads),
        out_shape=[jax.ShapeDtypeStruct((bsz, cdim), F32), jax.ShapeDtypeStruct((kc - 1, bsz, cdim), F32),
                   jax.ShapeDtypeStruct((bsz, width), F32), jax.ShapeDtypeStruct((bsz, width), F32)],
        compiler_params=pltpu.CompilerParams(vmem_limit_bytes=VMEM_LIMIT_BYTES),
    )(xbc, dt, conv_state.transpose(1, 0, 2), conv_w, conv_b.reshape(1, cdim), _pad_lanes(dt_bias),
      _pad_lanes(a_log))
    pad_t = lambda m: jnp.pad(m, ((0, LANES - bsz), (0, 0))).T
    per_step = math.gcd(bsz, SSD_STEP_BATCH)
    y, st = pl.pallas_call(
        functools.partial(_ssd_state_kernel, n_heads=n_heads),
        grid=(bsz // per_step,),
        in_specs=[pl.BlockSpec((per_step, width, n_state), lambda b: (b, 0, 0)),
                  _resident((width, LANES)), _resident((width, LANES)), _resident((bsz, cdim)),
                  _resident((bsz, width)), _resident((1, width)), _resident((1, width))],
        out_specs=[pl.BlockSpec((per_step, 1, width), lambda b: (b, 0, 0)),
                   pl.BlockSpec((per_step, width, n_state), lambda b: (b, 0, 0))],
        out_shape=[jax.ShapeDtypeStruct((bsz, 1, width), F32), jax.ShapeDtypeStruct(h0.shape, F32)],
        compiler_params=_params("arbitrary"),
    )(h0, pad_t(dtx), pad_t(dax), conv, z, jnp.repeat(d_skip, SSD_HEAD_DIM).reshape(1, width),
      norm_w.reshape(1, width))
    return y.reshape(bsz, width), new_conv.transpose(1, 0, 2), st


def _diff_lambda(lq1_ref, lk1_ref, lq2_ref, lk2_ref, lam_init):
    s1 = jnp.sum(lq1_ref[...] * lk1_ref[...], axis=-1, keepdims=True)
    s2 = jnp.sum(lq2_ref[...] * lk2_ref[...], axis=-1, keepdims=True)
    return jnp.exp(s1) - jnp.exp(s2) + lam_init


def _flash_kernel(q_ref, k_ref, vt_ref, lq1_ref, lk1_ref, lq2_ref, lk2_ref, g_ref, o_ref, *, lam_init):
    tq, tk, dh = ATTN_TQ, ATTN_TK, DA_HEAD_DIM
    qi = pl.program_id(2)
    q = q_ref[...]
    lane = lax.broadcasted_iota(jnp.int32, q.shape, 1)
    zero = jnp.zeros_like(q)
    qs = (jnp.where(lane < dh, q, zero), jnp.where(lane >= dh, q, zero))

    sw = ATTN_Q_STRIPE
    chains = [(c, r) for r in range(tq // sw) for c in range(2)]
    q_parts = [qs[c][r * sw:(r + 1) * sw, :] for c, r in chains]

    def absorb(blocks, carry):
        def block_scores(j, masked):
            kb = k_ref[pl.ds(pl.multiple_of(j * tk, tk), tk), :]
            out = []
            for (c, r), qp in zip(chains, q_parts):
                nk = min(tk, (r + 1) * sw) if masked else tk
                out.append(lax.dot_general(kb[0:nk], qp, (((1,), (1,)), ((), ())), preferred_element_type=F32))
            return out

        scores = [block_scores(*blocks[0])]
        for i, (j, masked) in enumerate(blocks):
            if i + 1 < len(blocks):
                scores.append(block_scores(*blocks[i + 1]))
            vt = jnp.concatenate([vt_ref[j], ones], axis=0)
            out = []
            for (c, r), s, (m, acc) in zip(chains, scores[i], carry):
                nk = s.shape[0]
                if masked:
                    keys = lax.broadcasted_iota(jnp.int32, (nk, sw), 0)
                    queries = lax.broadcasted_iota(jnp.int32, (nk, sw), 1) + r * sw
                    s = jnp.where(keys <= queries, s, NEG_BIG)
                m_new = jnp.maximum(m, jnp.max(s, axis=0, keepdims=True))
                alpha = jnp.exp2(m - m_new)
                p = jnp.exp2(s - m_new)
                acc = alpha * acc + jnp.dot(vt[:, 0:nk], p.astype(BF16), preferred_element_type=F32)
                out.append((m_new, acc))
            carry = tuple(out)
        return carry

    def finish(carry):
        a0, a1 = (jnp.concatenate([carry[i][1] for i, (c, _) in enumerate(chains) if c == cc], axis=1)
                  for cc in range(2))
        l0, l1 = a0[2 * dh:2 * dh + 1], a1[2 * dh:2 * dh + 1]
        lam = _diff_lambda(lq1_ref, lk1_ref, lq2_ref, lk2_ref, lam_init)
        o = (a0[0:2 * dh] * (1.0 / l0) - lam * (a1[0:2 * dh] * (1.0 / l1))).T
        o_ref[...] = (_rms_norm(o, g_ref[...]) * (1.0 - lam_init)).astype(o_ref.dtype)

    n = ATTN_BLOCKS_PER_TRIP
    ones = jnp.ones((ATTN_ONES_ROWS, tk), BF16)
    init = tuple((jnp.full((1, sw), NEG_BIG, F32), jnp.zeros((2 * dh + ATTN_ONES_ROWS, sw), F32))
                 for _ in chains)
    carry = lax.fori_loop(0, qi // n, lambda p, c: absorb([(n * p + i, False) for i in range(n)], c), init)

    for rem in range(n):
        @pl.when(qi % n == rem)
        def _(rem=rem):
            finish(absorb([(qi - rem + i, False) for i in range(rem)] + [(qi, True)], carry))


def _flash(qb, kb, vt, lq1, lk1, lq2, lk2, subln_w, lam_init):
    assert ATTN_TQ == ATTN_TK == ROW_TILE
    bsz, seq, width = qb.shape
    hw = 2 * DA_HEAD_DIM
    nk = seq // ATTN_TK
    vec = lambda a: a.reshape(1, -1)
    kv = pl.BlockSpec((None, seq, hw), lambda b, h, i: (b, 0, h))
    vts = pl.BlockSpec((nk, hw, ATTN_TK), lambda b, h, i: (b, h, 0))
    qo = pl.BlockSpec((None, ATTN_TQ, hw), lambda b, h, i: (b, i, h))
    small = _resident((1, DA_HEAD_DIM))
    return pl.pallas_call(
        functools.partial(_flash_kernel, lam_init=lam_init),
        grid=(bsz, width // hw, seq // ATTN_TQ),
        in_specs=[qo, kv, vts, small, small, small, small, _resident((1, hw))],
        out_specs=qo,
        out_shape=jax.ShapeDtypeStruct((bsz, seq, width), BF16),
        compiler_params=_params("parallel", "parallel", "arbitrary"),
    )(qb, kb, vt, vec(lq1), vec(lk1), vec(lq2), vec(lk2), vec(subln_w))


def _decode_attn_kernel(pt_ref, q_ref, kn_ref, vn_ref, lq1_ref, lk1_ref, lq2_ref, lk2_ref, g_ref, *rest,
                        lam_init, n_pages):
    k_refs = rest[0:n_pages]
    v_refs = rest[n_pages:2 * n_pages]
    o_ref, m_ref, l_ref, acc_ref = rest[2 * n_pages:]
    b, s = pl.program_id(0), pl.program_id(1)
    width = q_ref.shape[1]
    n_sub = width // DA_HEAD_DIM
    hw = 2 * DA_HEAD_DIM
    rows = lax.broadcasted_iota(jnp.int32, (n_sub, width), 0)
    lanes = lax.broadcasted_iota(jnp.int32, (n_sub, width), 1)
    qrow = q_ref[pl.ds(b, 1), :]
    qm = jnp.where(lanes // DA_HEAD_DIM == rows, jnp.broadcast_to(qrow, (n_sub, width)), 0.0).astype(BF16)

    @pl.when(s == 0)
    def _():
        kn = jnp.broadcast_to(kn_ref[pl.ds(b, 1), :], (SUBLANES, width))
        m_ref[...] = _dot_nt(qm, kn)[:, 0:1]
        l_ref[...] = jnp.ones_like(l_ref)
        acc_ref[...] = jnp.broadcast_to(vn_ref[pl.ds(b, 1), :], (n_sub, width))

    n_heads = width // hw
    page = k_refs[0].shape[0] // n_heads
    def by_position(ref):
        return jnp.concatenate([ref[pl.ds(h, page, stride=n_heads), :].astype(BF16) for h in range(n_heads)], axis=1)

    scores = [_dot_nt(qm, by_position(k_ref)) for k_ref in k_refs]
    m_old = m_ref[...]
    m_new = m_old
    for sc in scores:
        m_new = jnp.maximum(m_new, jnp.max(sc, axis=-1, keepdims=True))
    alpha = jnp.exp2(m_old - m_new)
    l = alpha * l_ref[...]
    acc = alpha * acc_ref[...]
    for sc, v_ref in zip(scores, v_refs):
        p = jnp.exp2(sc - m_new)
        l = l + jnp.sum(p, axis=-1, keepdims=True)
        acc = acc + _dot(p, by_position(v_ref))
    acc_ref[...] = acc
    m_ref[...] = m_new
    l_ref[...] = l

    @pl.when(s == pl.num_programs(1) - 1)
    def _():
        lam = _diff_lambda(lq1_ref, lk1_ref, lq2_ref, lk2_ref, lam_init)
        o = acc_ref[...] / l
        own = lanes // hw == rows // 2
        coef = jnp.where(rows % 2 == 0, 1.0, -lam)
        out = jnp.sum(jnp.where(own, coef * o, 0.0), axis=0, keepdims=True)
        g = g_ref[...]
        parts = [_rms_norm(out[:, h * hw:(h + 1) * hw], g) for h in range(width // hw)]
        o_ref[...] = jnp.concatenate(parts, axis=1) * (1.0 - lam_init)


def _decode_attn(page_table, q, k_new, v_new, cache_k, cache_v, layer, lq1, lk1, lq2, lk2, subln_w, lam_init):
    bsz, width = q.shape
    n_tab = page_table.shape[1]
    page_rows = cache_k.shape[2]
    npg = DEC_PAGES_PER_STEP
    assert n_tab % npg == 0
    hw = 2 * DA_HEAD_DIM
    vec = lambda a: a.reshape(1, -1)
    res = lambda shape: pl.BlockSpec(shape, lambda b, s, pt: (0,) * len(shape))

    def page_spec(i):
        return pl.BlockSpec((None, None, page_rows, hw),
                            lambda b, s, pt: (layer, pt[b * n_tab + s * npg + i], 0, 0))

    pages = [page_spec(i) for i in range(npg)]
    n_sub = width // DA_HEAD_DIM
    grid_spec = pltpu.PrefetchScalarGridSpec(
        num_scalar_prefetch=1,
        grid=(bsz, n_tab // npg),
        in_specs=[res((bsz, width))] * 3 + [res((1, DA_HEAD_DIM))] * 4 + [res((1, hw))] + pages + pages,
        out_specs=pl.BlockSpec((None, 1, width), lambda b, s, pt: (b, 0, 0)),
        scratch_shapes=[pltpu.VMEM((n_sub, 1), F32), pltpu.VMEM((n_sub, 1), F32), pltpu.VMEM((n_sub, width), F32)],
    )
    out = pl.pallas_call(
        functools.partial(_decode_attn_kernel, lam_init=lam_init, n_pages=npg),
        grid_spec=grid_spec,
        out_shape=jax.ShapeDtypeStruct((bsz, 1, width), F32),
        compiler_params=_params("arbitrary", "arbitrary"),
    )(page_table.reshape(-1), q, k_new, v_new, vec(lq1), vec(lk1), vec(lq2), vec(lk2), vec(subln_w),
      *([cache_k] * npg), *([cache_v] * npg))
    return out.reshape(bsz, width)


def kernel(x_prompt, x_sample, state_s5_re, state_s5_im, state_ssd, state_conv, cache_k, cache_v, page_table, p_prompt, p_sample, ln1_g, ln1_b, ln2_g, ln2_b, ssm_w_in, s5_lam_re, s5_lam_im, s5_log_step, s5_b_re, s5_b_im, s5_c_re, s5_c_im, s5_d, s5_w_glu, s5_b_glu, ssd_conv_w, ssd_conv_b, ssd_dt_bias, ssd_a_log, ssd_d, ssd_norm_w, ssm_w_out, attn_w_qkv, attn_lq1, attn_lk1, attn_lq2, attn_lk2, attn_subln_w, attn_w_out, mlp_w1, mlp_w2, ple_w_proj, ple_w_gate):
    bp, seq_p, d_model = x_prompt.shape
    bs, seq_s, _ = x_sample.shape
    assert seq_s == 1
    depth = ln1_g.shape[0]
    alpha = (2 * depth) ** 0.25
    s5_w = s5_d.shape[1]
    s5_groups, s5_state = s5_lam_re.shape[1:]
    ns = s5_groups * s5_state
    ssd_heads = ssd_dt_bias.shape[1]
    ssd_w = ssd_heads * SSD_HEAD_DIM
    cdim = ssd_conv_w.shape[2]
    kc = ssd_conv_w.shape[1]
    n_state = state_ssd.shape[-1]
    da_w = attn_w_out.shape[1]
    da_heads = da_w // (2 * DA_HEAD_DIM)
    page = cache_k.shape[2]
    past_len = page_table.shape[1] * page
    tp = bp * seq_p

    xp = x_prompt.reshape(tp, d_model)
    xs = x_sample.reshape(bs, d_model)
    tables_p = _rope_tables(jnp.arange(seq_p))
    tables_s = _rope_tables(jnp.full((bs,), past_len))
    ck = cache_k.reshape(cache_k.shape[0], cache_k.shape[1], page * da_heads, 2 * DA_HEAD_DIM)
    cv = cache_v.reshape(cache_v.shape[0], cache_v.shape[1], page * da_heads, 2 * DA_HEAD_DIM)

    outs = {name: [] for name in ("s5p", "s5s", "ssdp", "ssds", "convp", "convs", "kp", "vp", "ks", "vs")}
    for i in range(depth):
        j = i // 2
        if i % 2 == 0:
            n_main = s5_w + ssd_w + cdim
            w_in = [ssm_w_in[j][:, :n_main].astype(BF16),
                    jnp.pad(ssm_w_in[j][:, n_main:], ((0, 0), (0, LANES - ssd_heads))).astype(BF16)]
            splits = ((s5_w, ssd_w, cdim), (LANES,))
            a, bmat, cmat = _s5_prep(s5_lam_re[j], s5_lam_im[j], s5_log_step[j], s5_b_re[j], s5_b_im[j],
                                     s5_c_re[j], s5_c_im[j])
            wg = s5_w_glu[j].astype(BF16)
            ssd_prm = (ssd_conv_w[j], ssd_conv_b[j], ssd_dt_bias[j], ssd_a_log[j], ssd_d[j], ssd_norm_w[j])

            u, z, xbc, dt = _proj(xp, w_in, splits)
            ya, hfin = _s5_scan(u.reshape(bp, seq_p, s5_w), jnp.zeros((bp, 1, 2 * ns), F32), a, bmat, cmat,
                                s5_d[j], wg, s5_b_glu[j])
            yb, conv_p, st_p = _ssd_scan(z.reshape(bp, seq_p, ssd_w), xbc.reshape(bp, seq_p, cdim),
                                         dt.reshape(bp, seq_p, LANES), jnp.zeros((bp, kc - 1, cdim), F32),
                                         jnp.zeros((bp, ssd_w, n_state), F32), *ssd_prm)
            mixes_p = [ya.reshape(tp, s5_w), yb.reshape(tp, ssd_w)]
            outs["s5p"].append(hfin.reshape(bp, 2, s5_groups, s5_state))
            outs["ssdp"].append(st_p.reshape(bp, ssd_heads, SSD_HEAD_DIM, n_state))
            outs["convp"].append(conv_p)

            u, z, xbc, dt = _proj(xs, w_in, splits)
            h0 = jnp.concatenate([state_s5_re[j].reshape(bs, ns), state_s5_im[j].reshape(bs, ns)], axis=1)
            ya, hnew = _s5_step(u, h0, a, bmat, cmat, s5_d[j], wg, s5_b_glu[j])
            yb, conv_s, st_s = _ssd_step(z, xbc, dt, state_conv[j], state_ssd[j].reshape(bs, ssd_w, n_state),
                                         *ssd_prm)
            mixes_s = [ya, yb]
            outs["s5s"].append(hnew.reshape(bs, 2, s5_groups, s5_state))
            outs["ssds"].append(st_s.reshape(bs, ssd_heads, SSD_HEAD_DIM, n_state))
            outs["convs"].append(conv_s)
            w_out = ssm_w_out[j].astype(BF16)
            w_outs = [w_out[0:s5_w], w_out[s5_w:]]
        else:
            lam_init = 0.8 - 0.6 * math.exp(-0.3 * i)
            w_qkv = attn_w_qkv[j].astype(BF16)
            lam_prm = (attn_lq1[j], attn_lk1[j], attn_lq2[j], attn_lk2[j], attn_subln_w[j])

            k, v, qb, kb, vt = _qkv(xp, w_qkv, tables_p, True)
            shape3 = (bp, seq_p, da_w)
            o = _flash(qb.reshape(shape3), kb.reshape(shape3), vt, *lam_prm, lam_init)
            mixes_p = [o.reshape(tp, da_w)]
            outs["kp"].append(k.reshape(bp, seq_p, da_heads, 2 * DA_HEAD_DIM))
            outs["vp"].append(v.reshape(bp, seq_p, da_heads, 2 * DA_HEAD_DIM))

            k, v, qf = _qkv(xs, w_qkv, tables_s, False)
            o = _decode_attn(page_table, qf, k.reshape(bs, da_w), v.reshape(bs, da_w), ck, cv, j, *lam_prm,
                             lam_init)
            mixes_s = [o]
            outs["ks"].append(k.reshape(bs, 1, da_heads, 2 * DA_HEAD_DIM))
            outs["vs"].append(v.reshape(bs, 1, da_heads, 2 * DA_HEAD_DIM))
            w_outs = [attn_w_out[j].astype(BF16)]

        tail = (ln1_g[i], ln1_b[i], ln2_g[i], ln2_b[i], mlp_w1[i].astype(BF16), mlp_w2[i].astype(BF16),
                ple_w_proj[i].astype(BF16), ple_w_gate[i].astype(BF16), alpha)
        xp = _finish(xp, mixes_p, w_outs, p_prompt[i].reshape(tp, -1), *tail)
        xs = _finish(xs, mixes_s, w_outs, p_sample[i].reshape(bs, -1), *tail)

    s5p, s5s = jnp.stack(outs["s5p"]), jnp.stack(outs["s5s"])
    return (xp.reshape(bp, seq_p, d_model), xs.reshape(bs, 1, d_model),
            s5p[:, :, 0], s5p[:, :, 1], s5s[:, :, 0], s5s[:, :, 1],
            jnp.stack(outs["ssdp"]), jnp.stack(outs["ssds"]), jnp.stack(outs["convp"]), jnp.stack(outs["convs"]),
            jnp.stack(outs["kp"]), jnp.stack(outs["vp"]), jnp.stack(outs["ks"]), jnp.stack(outs["vs"]))
```

```python
import functools
import math

import jax
import jax.numpy as jnp
from jax import lax
from jax.experimental import pallas as pl
from jax.experimental.pallas import tpu as pltpu

F32 = jnp.float32
BF16 = jnp.bfloat16

S5_GROUP = 16
SSD_HEAD_DIM = 64
SSD_GROUPS = 2
DA_HEAD_DIM = 64
ROT_DIM = DA_HEAD_DIM // 4
ROPE_THETA = 500000.0
LN_EPS = 1e-5
RMS_EPS = 1e-5

LANES = 128
SUBLANES = 8
VMEM_LIMIT_BYTES = 56 * 1024 * 1024

ROW_TILE = 512
S5_SEGMENTS = SUBLANES
S5_SEG_LEN = 64
S5_LANE_CHUNK = 512
SSD_CHUNK = 256
SSD_STEP_BATCH = 4
ATTN_TQ = 512
ATTN_TK = 512
ATTN_Q_STRIPE = 256
ATTN_BLOCKS_PER_TRIP = 8
ATTN_ONES_ROWS = 16
DEC_PAGES_PER_STEP = 16
NEG_BIG = -1e30


def _resident(shape):
    zeros = (0,) * len(shape)
    return pl.BlockSpec(shape, lambda *_: zeros, pipeline_mode=pl.Buffered(1))


def _params(*sem):
    return pltpu.CompilerParams(dimension_semantics=sem, vmem_limit_bytes=VMEM_LIMIT_BYTES)


def _dot(a, b):
    return jnp.dot(a.astype(BF16), b.astype(BF16), preferred_element_type=F32)


def _dot_nt(a, b):
    return lax.dot_general(a.astype(BF16), b.astype(BF16), (((1,), (1,)), ((), ())),
                           preferred_element_type=F32)


def _split3(x):
    hi = x.astype(BF16)
    r = x - hi.astype(F32)
    mid = r.astype(BF16)
    lo = (r - mid.astype(F32)).astype(BF16)
    return hi, mid, lo


def _dot_exact(a, b, selector):
    sel = (a, b)[selector].astype(BF16)
    pieces = _split3((b, a)[selector])
    dot = (lambda p: jnp.dot(sel, p, preferred_element_type=F32)) if selector == 0 else \
          (lambda p: jnp.dot(p, sel, preferred_element_type=F32))
    return dot(pieces[0]) + dot(pieces[1]) + dot(pieces[2])


def _sigmoid(x):
    return 0.5 * jnp.tanh(0.5 * x) + 0.5


def _silu(x):
    return x * _sigmoid(x)


def _layer_norm(x, g, b):
    mu = jnp.mean(x, -1, keepdims=True)
    xc = x - mu
    var = jnp.mean(xc * xc, -1, keepdims=True)
    return xc * lax.rsqrt(var + LN_EPS) * g + b


def _rms_norm(x, g):
    return x * lax.rsqrt(jnp.mean(x * x, -1, keepdims=True) + RMS_EPS) * g


def _row_tile(m):
    return ROW_TILE if m % ROW_TILE == 0 else m


def _proj_kernel(x_ref, *refs, splits):
    w_refs, o_refs = refs[:len(splits)], refs[len(splits):]
    xb = x_ref[...].astype(BF16)
    outs = iter(o_refs)
    for w_ref, widths in zip(w_refs, splits):
        off = 0
        for n in widths:
            next(outs)[...] = jnp.dot(xb, w_ref[:, off:off + n], preferred_element_type=F32)
            off += n


def _proj(x, ws, splits):
    m, k = x.shape
    tm = _row_tile(m)
    widths = [n for group in splits for n in group]
    return pl.pallas_call(
        functools.partial(_proj_kernel, splits=splits),
        grid=(m // tm,),
        in_specs=[pl.BlockSpec((tm, k), lambda i: (i, 0))] + [_resident(w.shape) for w in ws],
        out_specs=[pl.BlockSpec((tm, n), lambda i: (i, 0)) for n in widths],
        out_shape=[jax.ShapeDtypeStruct((m, n), F32) for n in widths],
        compiler_params=_params("parallel"),
    )(x, *ws)


def _rope(t, ca, sa, sb):
    half = ROT_DIM // 2
    outs = []
    for h in range(t.shape[1] // LANES):
        c = t[:, h * LANES:(h + 1) * LANES]
        up = pltpu.roll(c, LANES - half, 1)
        dn = pltpu.roll(c, half, 1)
        outs.append(c * ca + up * sa + dn * sb)
    return jnp.concatenate(outs, axis=1)


def _qkv_rows(x_ref, w_ref, ca_ref, sa_ref, sb_ref, k_ref, v_ref, *, width, scale):
    xb = x_ref[...].astype(BF16)
    ca, sa, sb = ca_ref[...], sa_ref[...], sb_ref[...]
    q = _rope(jnp.dot(xb, w_ref[:, 0:width], preferred_element_type=F32), ca, sa, sb) * scale
    k = _rope(jnp.dot(xb, w_ref[:, width:2 * width], preferred_element_type=F32), ca, sa, sb)
    v = jnp.dot(xb, w_ref[:, 2 * width:3 * width], preferred_element_type=F32)
    tm = xb.shape[0]
    n_heads = width // LANES
    for h in range(n_heads):
        k_ref[pl.ds(h, tm, stride=n_heads), :] = k[:, h * LANES:(h + 1) * LANES]
        v_ref[pl.ds(h, tm, stride=n_heads), :] = v[:, h * LANES:(h + 1) * LANES]
    return q, k, v


def _qkv_prompt_kernel(x_ref, w_ref, ca_ref, sa_ref, sb_ref, k_ref, v_ref, qb_ref, kb_ref, vt_ref, **kw):
    q, k, v = _qkv_rows(x_ref, w_ref, ca_ref, sa_ref, sb_ref, k_ref, v_ref, **kw)
    qb_ref[...] = q.astype(BF16)
    kb_ref[...] = k.astype(BF16)
    vt_ref[...] = v.T.astype(BF16)


def _qkv_decode_kernel(x_ref, w_ref, ca_ref, sa_ref, sb_ref, k_ref, v_ref, q_ref, **kw):
    q, _, _ = _qkv_rows(x_ref, w_ref, ca_ref, sa_ref, sb_ref, k_ref, v_ref, **kw)
    q_ref[...] = q


def _rope_tables(pos):
    half = ROT_DIM // 2
    inv = ROPE_THETA ** (-jnp.arange(0, ROT_DIM, 2, dtype=F32) / ROT_DIM)
    ang = pos.astype(F32)[:, None] * inv
    cos, sin = jnp.cos(ang), jnp.sin(ang)
    n = pos.shape[0]
    pad = jnp.zeros((n, DA_HEAD_DIM - ROT_DIM), F32)
    zero = jnp.zeros((n, half), F32)
    ca = jnp.concatenate([cos, cos, pad + 1.0], axis=1)
    sa = jnp.concatenate([-sin, zero, pad], axis=1)
    sb = jnp.concatenate([zero, sin, pad], axis=1)
    rep = LANES // DA_HEAD_DIM
    return jnp.tile(ca, (1, rep)), jnp.tile(sa, (1, rep)), jnp.tile(sb, (1, rep))


def _qkv(x, w, tables, for_flash):
    m, d = x.shape
    width = w.shape[1] // 3
    n_heads = width // LANES
    tm = _row_tile(m)
    nt = tables[0].shape[0] // tm
    row = lambda i: (i, 0)
    tab = pl.BlockSpec((tm, LANES), lambda i: (i % nt, 0))
    full = pl.BlockSpec((tm, width), row)
    native = pl.BlockSpec((tm * n_heads, LANES), row)
    native_shape = jax.ShapeDtypeStruct((m * n_heads, LANES), F32)
    if for_flash:
        body = _qkv_prompt_kernel
        out_specs = [native, native, full, full, pl.BlockSpec((None, width, tm), lambda i: (i, 0, 0))]
        out_shape = [native_shape, native_shape, jax.ShapeDtypeStruct((m, width), BF16),
                     jax.ShapeDtypeStruct((m, width), BF16), jax.ShapeDtypeStruct((m // tm, width, tm), BF16)]
    else:
        body = _qkv_decode_kernel
        out_specs = [native, native, full]
        out_shape = [native_shape, native_shape, jax.ShapeDtypeStruct((m, width), F32)]
    return pl.pallas_call(
        functools.partial(body, width=width, scale=DA_HEAD_DIM ** -0.5 * math.log2(math.e)),
        grid=(m // tm,),
        in_specs=[pl.BlockSpec((tm, d), row), _resident(w.shape), tab, tab, tab],
        out_specs=out_specs,
        out_shape=out_shape,
        compiler_params=_params("parallel"),
    )(x, w, *tables)


def _finish_kernel(*refs, n_mix, alpha, ff_chunk):
    x_ref = refs[0]
    y_refs = refs[1:1 + n_mix]
    wo_refs = refs[1 + n_mix:1 + 2 * n_mix]
    (p_ref, g1_ref, b1_ref, g2_ref, b2_ref, w1_ref, w2_ref, wp_ref, wg_ref, o_ref) = refs[1 + 2 * n_mix:]
    mix = _dot(y_refs[0][...], wo_refs[0][...])
    for y_ref, wo_ref in zip(y_refs[1:], wo_refs[1:]):
        mix = mix + _dot(y_ref[...], wo_ref[...])
    x1 = _layer_norm(alpha * x_ref[...] + mix, g1_ref[...], b1_ref[...])
    x1b = x1.astype(BF16)
    h = None
    for c in range(0, w1_ref.shape[1], ff_chunk):
        a = jnp.maximum(jnp.dot(x1b, w1_ref[:, c:c + ff_chunk], preferred_element_type=F32), 0.0)
        part = jnp.dot((a * a).astype(BF16), w2_ref[c:c + ff_chunk, :], preferred_element_type=F32)
        h = part if h is None else h + part
    x2 = _layer_norm(alpha * x1 + h, g2_ref[...], b2_ref[...])
    gate = _sigmoid(_dot(x2, wg_ref[...]))
    o_ref[...] = x2 + gate * _dot(p_ref[...], wp_ref[...])


def _finish(x, mixes, w_outs, p, g1, b1, g2, b2, w1, w2, wp, wg, alpha):
    m, d = x.shape
    tm = _row_tile(m)
    row = lambda i: (i, 0)
    vec = lambda a: a.reshape(1, d)
    in_specs = ([pl.BlockSpec((tm, d), row)]
                + [pl.BlockSpec((tm, y.shape[1]), row) for y in mixes]
                + [_resident(w.shape) for w in w_outs]
                + [pl.BlockSpec((tm, p.shape[1]), row)]
                + [_resident((1, d))] * 4
                + [_resident(w.shape) for w in (w1, w2, wp, wg)])
    return pl.pallas_call(
        functools.partial(_finish_kernel, n_mix=len(mixes), alpha=alpha, ff_chunk=1024),
        grid=(m // tm,),
        in_specs=in_specs,
        out_specs=pl.BlockSpec((tm, d), row),
        out_shape=jax.ShapeDtypeStruct((m, d), F32),
        compiler_params=_params("parallel"),
    )(x, *mixes, *w_outs, p, vec(g1), vec(b1), vec(g2), vec(b2), w1, w2, wp, wg)


def _s5_prep_kernel(lr_ref, li_ref, ls_ref, br_ref, bi_ref, ar_ref, ai_ref, bbr_ref, bbi_ref):
    lr, li = lr_ref[...], li_ref[...]
    step = jnp.exp(ls_ref[...])
    mag = jnp.exp(lr * step)
    a_re = mag * jnp.cos(li * step)
    a_im = mag * jnp.sin(li * step)
    den = lr * lr + li * li
    coef_re = ((a_re - 1.0) * lr + a_im * li) / den
    coef_im = (a_im * lr - (a_re - 1.0) * li) / den
    br, bi = br_ref[...], bi_ref[...]
    ar_ref[...] = a_re
    ai_ref[...] = a_im
    bbr_ref[...] = coef_re * br - coef_im * bi
    bbi_ref[...] = coef_re * bi + coef_im * br


def _s5_prep(lam_re, lam_im, log_step, b_re, b_im, c_re, c_im):
    g, p = lam_re.shape
    hh = b_re.shape[-1]
    a_re, a_im, bb_re, bb_im = pl.pallas_call(
        _s5_prep_kernel,
        out_shape=[jax.ShapeDtypeStruct((g, 1, p), F32)] * 2 + [jax.ShapeDtypeStruct((g, hh, p), F32)] * 2,
    )(lam_re.reshape(g, 1, p), lam_im.reshape(g, 1, p), log_step.reshape(g, 1, 1),
      b_re.transpose(0, 2, 1), b_im.transpose(0, 2, 1))
    a = jnp.concatenate([a_re.reshape(1, g * p), a_im.reshape(1, g * p)], axis=1)
    gt = LANES // hh
    nt = g // gt
    eye = jnp.eye(gt, dtype=bool)

    def blockdiag_in(bb):
        bb = bb.reshape(nt, gt, hh, p)
        return jnp.where(eye[None, :, None, :, None], bb[:, :, :, None, :], 0.0).reshape(nt, gt * hh, gt * p)

    def blockdiag_out(c):
        c = c.transpose(0, 2, 1).reshape(nt, gt, p, hh)
        return jnp.where(eye[None, :, None, :, None], c[:, :, :, None, :], 0.0).reshape(nt, gt * p, gt * hh)

    bmat = jnp.concatenate([blockdiag_in(bb_re), blockdiag_in(bb_im)], axis=2).astype(BF16)
    cmat = jnp.concatenate([blockdiag_out(c_re), -blockdiag_out(c_im)], axis=1).astype(BF16)
    return a, bmat, cmat


def _s5_in_map(u, bmat_ref, store):
    for t in range(bmat_ref.shape[0]):
        part = _dot(u[:, t * LANES:(t + 1) * LANES], bmat_ref[t])
        s = part.shape[1] // 2
        store(t * s, part[:, 0:s], part[:, s:2 * s])


def _s5_out_map(load, cmat_ref):
    ys = []
    for t in range(cmat_ref.shape[0]):
        s = cmat_ref.shape[1] // 2
        hr, hi = load(t * s, s)
        ys.append(_dot(hr, cmat_ref[t, 0:s, :]) + _dot(hi, cmat_ref[t, s:2 * s, :]))
    return jnp.concatenate(ys, axis=1)


def _s5_tail(y, u, d, wg, bg):
    y = jax.nn.gelu(y + d * u)
    return y * _sigmoid(_dot(y, wg) + bg)


def _s5_scan_kernel(u_ref, h0_ref, a_ref, bmat_ref, cmat_ref, d_ref, wg_ref, bg_ref,
                    y_ref, hout_ref, col_ref, up_ref, bu_ref, pw_ref, hin_ref, carry_ref):
    n_seg, seg = S5_SEGMENTS, S5_SEG_LEN
    ns = a_ref.shape[1] // 2
    lw = S5_LANE_CHUNK
    first_chunk = pl.program_id(1) == 0

    @pl.when(jnp.logical_and(pl.program_id(0) == 0, first_chunk))
    def _():
        ar, ai = a_ref[:, 0:ns], a_ref[:, ns:2 * ns]
        pw_ref[0:n_seg, :] = jnp.broadcast_to(a_ref[...], (n_seg, 2 * ns))

        def body(k, c):
            pr, pi = c
            nr, ni = pr * ar - pi * ai, pr * ai + pi * ar
            r0 = pl.multiple_of(k * n_seg, n_seg)
            pw_ref[pl.ds(r0, n_seg), 0:ns] = jnp.broadcast_to(nr, (n_seg, ns))
            pw_ref[pl.ds(r0, n_seg), ns:2 * ns] = jnp.broadcast_to(ni, (n_seg, ns))
            return nr, ni

        lax.fori_loop(1, seg, body, (ar, ai))

    @pl.when(first_chunk)
    def _():
        carry_ref[...] = h0_ref[...]

    n_col = u_ref.shape[1] // LANES
    for k in range(n_col):
        col_ref[k] = u_ref[:, k * LANES:(k + 1) * LANES]
    for t in range(seg):
        for k in range(n_col):
            up_ref[t * n_seg:(t + 1) * n_seg, k * LANES:(k + 1) * LANES] = col_ref[k, pl.ds(t, n_seg, stride=seg), :]
    def store_bu(off, re, im):
        bu_ref[:, off:off + re.shape[1]] = re
        bu_ref[:, ns + off:ns + off + im.shape[1]] = im

    _s5_in_map(up_ref[...], bmat_ref, store_bu)

    for lc in range(0, ns, lw):
        ar = jnp.broadcast_to(a_ref[:, lc:lc + lw], (n_seg, lw))
        ai = jnp.broadcast_to(a_ref[:, ns + lc:ns + lc + lw], (n_seg, lw))

        def sweep(t, c, lc=lc, ar=ar, ai=ai):
            hr, hi = c
            r0 = pl.multiple_of(t * n_seg, n_seg)
            nr = ar * hr - ai * hi + bu_ref[pl.ds(r0, n_seg), lc:lc + lw]
            ni = ar * hi + ai * hr + bu_ref[pl.ds(r0, n_seg), ns + lc:ns + lc + lw]
            bu_ref[pl.ds(r0, n_seg), lc:lc + lw] = nr
            bu_ref[pl.ds(r0, n_seg), ns + lc:ns + lc + lw] = ni
            return nr, ni

        zero = jnp.zeros((n_seg, lw), F32)
        lax.fori_loop(0, seg, sweep, (zero, zero), unroll=2)

    top = (seg - 1) * n_seg
    asr, asi = pw_ref[top:top + 1, 0:ns], pw_ref[top:top + 1, ns:2 * ns]
    cr, ci = carry_ref[:, 0:ns], carry_ref[:, ns:2 * ns]
    last = (seg - 1) * n_seg
    for s in range(n_seg):
        hin_ref[s:s + 1, 0:ns] = cr
        hin_ref[s:s + 1, ns:2 * ns] = ci
        er = bu_ref[last + s:last + s + 1, 0:ns]
        ei = bu_ref[last + s:last + s + 1, ns:2 * ns]
        cr, ci = asr * cr - asi * ci + er, asr * ci + asi * cr + ei
    carry_ref[:, 0:ns] = cr
    carry_ref[:, ns:2 * ns] = ci
    hout_ref[...] = carry_ref[...]

    for lc in range(0, ns, lw):
        hr_in = hin_ref[:, lc:lc + lw]
        hi_in = hin_ref[:, ns + lc:ns + lc + lw]

        def fix(t, c, lc=lc, hr_in=hr_in, hi_in=hi_in):
            r0 = pl.multiple_of(t * n_seg, n_seg)
            pr = pw_ref[pl.ds(r0, n_seg), lc:lc + lw]
            pi = pw_ref[pl.ds(r0, n_seg), ns + lc:ns + lc + lw]
            bu_ref[pl.ds(r0, n_seg), lc:lc + lw] += pr * hr_in - pi * hi_in
            bu_ref[pl.ds(r0, n_seg), ns + lc:ns + lc + lw] += pr * hi_in + pi * hr_in
            return c

        lax.fori_loop(0, seg, fix, 0, unroll=2)

    y = _s5_out_map(lambda off, s: (bu_ref[:, off:off + s], bu_ref[:, ns + off:ns + off + s]), cmat_ref)
    out = _s5_tail(y, up_ref[...], d_ref[...], wg_ref[...], bg_ref[...])
    for k in range(n_col):
        col_ref[k] = out[:, k * LANES:(k + 1) * LANES]
    for r0 in range(0, n_seg * seg, n_seg):
        s, t0 = divmod(r0, seg)
        for k in range(n_col):
            y_ref[r0:r0 + n_seg, k * LANES:(k + 1) * LANES] = col_ref[k, pl.ds(t0 * n_seg + s, n_seg, stride=n_seg), :]


def _s5_scan(u, h0, a, bmat, cmat, d, wg, bg):
    bsz, seq, w = u.shape
    q = S5_SEGMENTS * S5_SEG_LEN
    ns2 = a.shape[1]
    return pl.pallas_call(
        _s5_scan_kernel,
        grid=(bsz, seq // q),
        in_specs=[pl.BlockSpec((None, q, w), lambda b, c: (b, c, 0)),
                  pl.BlockSpec((None, 1, ns2), lambda b, c: (b, 0, 0)),
                  _resident(a.shape), _resident(bmat.shape), _resident(cmat.shape),
                  _resident((1, w)), _resident(wg.shape), _resident((1, w))],
        out_specs=[pl.BlockSpec((None, q, w), lambda b, c: (b, c, 0)),
                   pl.BlockSpec((None, 1, ns2), lambda b, c: (b, 0, 0))],
        out_shape=[jax.ShapeDtypeStruct((bsz, seq, w), F32), jax.ShapeDtypeStruct((bsz, 1, ns2), F32)],
        scratch_shapes=[pltpu.VMEM((w // LANES, q, LANES), F32), pltpu.VMEM((q, w), F32), pltpu.VMEM((q, ns2), F32),
                        pltpu.VMEM((q, ns2), F32), pltpu.VMEM((S5_SEGMENTS, ns2), F32),
                        pltpu.VMEM((1, ns2), F32)],
        compiler_params=_params("arbitrary", "arbitrary"),
    )(u, h0, a, bmat, cmat, d.reshape(1, w), wg, bg.reshape(1, w))


def _s5_step_kernel(u_ref, h0_ref, a_ref, bmat_ref, cmat_ref, d_ref, wg_ref, bg_ref, y_ref, hout_ref):
    ns = a_ref.shape[1] // 2
    u = u_ref[...]

    def advance(off, bu_re, bu_im):
        s = bu_re.shape[1]
        ar, ai = a_ref[:, off:off + s], a_ref[:, ns + off:ns + off + s]
        h0r, h0i = h0_ref[:, off:off + s], h0_ref[:, ns + off:ns + off + s]
        hout_ref[:, off:off + s] = ar * h0r - ai * h0i + bu_re
        hout_ref[:, ns + off:ns + off + s] = ar * h0i + ai * h0r + bu_im

    _s5_in_map(u, bmat_ref, advance)
    y = _s5_out_map(lambda off, s: (hout_ref[:, off:off + s], hout_ref[:, ns + off:ns + off + s]), cmat_ref)
    y_ref[...] = _s5_tail(y, u, d_ref[...], wg_ref[...], bg_ref[...])


def _s5_step(u, h0, a, bmat, cmat, d, wg, bg):
    bsz, w = u.shape
    return pl.pallas_call(
        _s5_step_kernel,
        out_shape=[jax.ShapeDtypeStruct((bsz, w), F32), jax.ShapeDtypeStruct(h0.shape, F32)],
        compiler_params=pltpu.CompilerParams(vmem_limit_bytes=VMEM_LIMIT_BYTES),
    )(u, h0, a, bmat, cmat, d.reshape(1, w), wg, bg.reshape(1, w))


def _head_expander(n_heads):
    rows = lax.broadcasted_iota(jnp.int32, (LANES, n_heads * SSD_HEAD_DIM), 0)
    cols = lax.broadcasted_iota(jnp.int32, (LANES, n_heads * SSD_HEAD_DIM), 1)
    return (cols // SSD_HEAD_DIM == rows).astype(F32)


def _ssd_scan_kernel(z_ref, xbc_ref, dt_ref, cs_ref, h0_ref, cw_ref, cb_ref, dtb_ref, alog_ref, dx_ref, nw_ref,
                     y_ref, cso_ref, ho_ref, xpad_ref, st_ref, *, n_heads):
    q = z_ref.shape[0]
    width = n_heads * SSD_HEAD_DIM
    n_state = (xbc_ref.shape[1] - width) // (2 * SSD_GROUPS)
    kc = cw_ref.shape[0]
    hg = n_heads // SSD_GROUPS
    gw = hg * SSD_HEAD_DIM
    c = pl.program_id(1)
    tail0 = SUBLANES - (kc - 1)

    @pl.when(c == 0)
    def _():
        xpad_ref[tail0:SUBLANES, :] = cs_ref[...]
        st_ref[...] = h0_ref[...].T

    @pl.when(c != 0)
    def _():
        xpad_ref[tail0:SUBLANES, :] = xpad_ref[q + tail0:q + SUBLANES, :]

    xbc = xbc_ref[...]
    xpad_ref[SUBLANES:SUBLANES + q, :] = xbc
    cso_ref[...] = xbc[q - (kc - 1):q, :]

    conv = cb_ref[...] + xbc * cw_ref[kc - 1:kc, :]
    for k in range(kc - 1):
        conv = conv + xpad_ref[tail0 + k:tail0 + k + q, :] * cw_ref[k:k + 1, :]
    conv = _silu(conv)
    xs = conv[:, 0:width]
    bm = [conv[:, width + g * n_state:width + (g + 1) * n_state] for g in range(SSD_GROUPS)]
    cm = [conv[:, width + (SSD_GROUPS + g) * n_state:width + (SSD_GROUPS + g + 1) * n_state]
          for g in range(SSD_GROUPS)]

    dt = jax.nn.softplus(dt_ref[...] + dtb_ref[...])
    av = dt * (-jnp.exp(alog_ref[...]))
    rows = lax.broadcasted_iota(jnp.int32, (q, q), 0)
    cols = lax.broadcasted_iota(jnp.int32, (q, q), 1)
    causal = rows >= cols
    a_cs = _dot_exact(causal.astype(F32), av, 0)
    a_cs_t = a_cs.T
    expand = _head_expander(n_heads)
    a_cs_x = _dot_exact(a_cs, expand, 1)
    tot_x = a_cs_x[q - 1:q, :]
    xdt = xs * _dot_exact(dt, expand, 1)
    cb = [_dot_nt(cm[g], bm[g]) for g in range(SSD_GROUPS)]

    y_parts = []
    for h in range(n_heads):
        g = h // hg
        seg = jnp.where(causal, a_cs[:, h:h + 1] - a_cs_t[h:h + 1, :], -jnp.inf)
        gmat = cb[g] * jnp.exp(seg)
        y_parts.append(_dot(gmat, xdt[:, h * SSD_HEAD_DIM:(h + 1) * SSD_HEAD_DIM]))
    y = jnp.concatenate(y_parts, axis=1)

    st = st_ref[...]
    y_off = jnp.concatenate([_dot(cm[g], st[:, g * gw:(g + 1) * gw]) for g in range(SSD_GROUPS)], axis=1)
    y = y + y_off * jnp.exp(a_cs_x) + dx_ref[...] * xs
    xdec = xdt * jnp.exp(tot_x - a_cs_x)
    upd = jnp.concatenate([_dot(bm[g].T, xdec[:, g * gw:(g + 1) * gw]) for g in range(SSD_GROUPS)], axis=1)
    st_new = st * jnp.exp(tot_x) + upd
    st_ref[...] = st_new

    @pl.when(c == pl.num_programs(1) - 1)
    def _():
        ho_ref[...] = st_new.T

    z = z_ref[...]
    y_ref[...] = _rms_norm(y * _silu(z), nw_ref[...]).astype(y_ref.dtype)


def _pad_lanes(v):
    return jnp.pad(v.reshape(1, -1), ((0, 0), (0, LANES - v.shape[-1])))


def _ssd_scan(z, xbc, dt, conv_state, h0, conv_w, conv_b, dt_bias, a_log, d_skip, norm_w):
    bsz, seq, width = z.shape
    cdim = xbc.shape[2]
    n_heads = dt_bias.shape[0]
    kc = conv_w.shape[0]
    n_state = h0.shape[2]
    q = SSD_CHUNK
    blk = lambda n: pl.BlockSpec((None, q, n), lambda b, c: (b, c, 0))
    per_b = lambda s: pl.BlockSpec((None,) + s, lambda b, c: (b, 0, 0))
    return pl.pallas_call(
        functools.partial(_ssd_scan_kernel, n_heads=n_heads),
        grid=(bsz, seq // q),
        in_specs=[blk(width), blk(cdim), blk(LANES), per_b((kc - 1, cdim)), per_b((width, n_state)),
                  _resident((kc, cdim)), _resident((1, cdim)), _resident((1, LANES)), _resident((1, LANES)),
                  _resident((1, width)), _resident((1, width))],
        out_specs=[blk(width), per_b((kc - 1, cdim)), per_b((width, n_state))],
        out_shape=[jax.ShapeDtypeStruct((bsz, seq, width), BF16),
                   jax.ShapeDtypeStruct((bsz, kc - 1, cdim), F32),
                   jax.ShapeDtypeStruct((bsz, width, n_state), F32)],
        scratch_shapes=[pltpu.VMEM((q + SUBLANES, cdim), F32), pltpu.VMEM((n_state, width), F32)],
        compiler_params=_params("arbitrary", "arbitrary"),
    )(z, xbc, dt, conv_state, h0, conv_w, conv_b.reshape(1, cdim), _pad_lanes(dt_bias), _pad_lanes(a_log),
      jnp.repeat(d_skip, SSD_HEAD_DIM).reshape(1, width), norm_w.reshape(1, width))


def _ssd_pre_kernel(xbc_ref, dt_ref, cs_ref, cw_ref, cb_ref, dtb_ref, alog_ref,
                    conv_ref, cso_ref, dtx_ref, dax_ref, *, n_heads):
    kc = cw_ref.shape[0]
    width = n_heads * SSD_HEAD_DIM
    xbc = xbc_ref[...]
    conv = cb_ref[...] + xbc * cw_ref[kc - 1:kc, :]
    for k in range(kc - 1):
        conv = conv + cs_ref[k] * cw_ref[k:k + 1, :]
    for k in range(kc - 2):
        cso_ref[k] = cs_ref[k + 1]
    cso_ref[kc - 2] = xbc
    conv = _silu(conv)
    conv_ref[...] = conv
    dt = jax.nn.softplus(dt_ref[...] + dtb_ref[...])
    expand = _head_expander(n_heads)
    dtx_ref[...] = conv[:, 0:width] * _dot_exact(dt, expand, 1)
    dax_ref[...] = jnp.exp(_dot_exact(dt * (-jnp.exp(alog_ref[...])), expand, 1))


def _ssd_state_kernel(st_ref, dtxt_ref, daxt_ref, conv_ref, z_ref, dx_ref, nw_ref, y_ref, sto_ref, *, n_heads):
    per_step, width, n_state = st_ref.shape
    gw = width // SSD_GROUPS
    nb = dtxt_ref.shape[1]
    for i in range(per_step):
        b = pl.program_id(0) * per_step + i
        onehot = (lax.broadcasted_iota(jnp.int32, (nb, n_state), 0) == b).astype(F32)
        dtx_col = _dot_exact(dtxt_ref[...], onehot, 1)
        da_col = _dot_exact(daxt_ref[...], onehot, 1)
        row = conv_ref[pl.ds(b, 1), :]
        ys = []
        for g in range(SSD_GROUPS):
            r = slice(g * gw, (g + 1) * gw)
            bm = row[:, width + g * n_state:width + (g + 1) * n_state]
            cm = row[:, width + (SSD_GROUPS + g) * n_state:width + (SSD_GROUPS + g + 1) * n_state]
            new = st_ref[i, r, :] * da_col[r, :] + dtx_col[r, :] * bm
            sto_ref[i, r, :] = new
            ys.append(_dot_nt(jnp.broadcast_to(cm, (SUBLANES, n_state)), new)[0:1, :])
        xs = row[:, 0:width]
        y = jnp.concatenate(ys, axis=1) + dx_ref[...] * xs
        y_ref[i] = _rms_norm(y * _silu(z_ref[pl.ds(b, 1), :]), nw_ref[...])


def _ssd_step(z, xbc, dt, conv_state, h0, conv_w, conv_b, dt_bias, a_log, d_skip, norm_w):
    bsz, width = z.shape
    cdim = xbc.shape[1]
    n_heads = dt_bias.shape[0]
    kc = conv_w.shape[0]
    n_state = h0.shape[2]
    conv, new_conv, dtx, dax = pl.pallas_call(
        functools.partial(_ssd_pre_kernel, n_heads=n_heads),
        out_shape=[jax.ShapeDtypeStruct((bsz, cdim), F32), jax.ShapeDtypeStruct((kc - 1, bsz, cdim), F32),
                   jax.ShapeDtypeStruct((bsz, width), F32), jax.ShapeDtypeStruct((bsz, width), F32)],
        compiler_params=pltpu.CompilerParams(vmem_limit_bytes=VMEM_LIMIT_BYTES),
    )(xbc, dt, conv_state.transpose(1, 0, 2), conv_w, conv_b.reshape(1, cdim), _pad_lanes(dt_bias),
      _pad_lanes(a_log))
    pad_t = lambda m: jnp.pad(m, ((0, LANES - bsz), (0, 0))).T
    per_step = math.gcd(bsz, SSD_STEP_BATCH)
    y, st = pl.pallas_call(
        functools.partial(_ssd_state_kernel, n_heads=n_heads),
        grid=(bsz // per_step,),
        in_specs=[pl.BlockSpec((per_step, width, n_state), lambda b: (b, 0, 0)),
                  _resident((width, LANES)), _resident((width, LANES)), _resident((bsz, cdim)),
                  _resident((bsz, width)), _resident((1, width)), _resident((1, width))],
        out_specs=[pl.BlockSpec((per_step, 1, width), lambda b: (b, 0, 0)),
                   pl.BlockSpec((per_step, width, n_state), lambda b: (b, 0, 0))],
        out_shape=[jax.ShapeDtypeStruct((bsz, 1, width), F32), jax.ShapeDtypeStruct(h0.shape, F32)],
        compiler_params=_params("arbitrary"),
    )(h0, pad_t(dtx), pad_t(dax), conv, z, jnp.repeat(d_skip, SSD_HEAD_DIM).reshape(1, width),
      norm_w.reshape(1, width))
    return y.reshape(bsz, width), new_conv.transpose(1, 0, 2), st


def _diff_lambda(lq1_ref, lk1_ref, lq2_ref, lk2_ref, lam_init):
    s1 = jnp.sum(lq1_ref[...] * lk1_ref[...], axis=-1, keepdims=True)
    s2 = jnp.sum(lq2_ref[...] * lk2_ref[...], axis=-1, keepdims=True)
    return jnp.exp(s1) - jnp.exp(s2) + lam_init


def _flash_kernel(q_ref, k_ref, vt_ref, lq1_ref, lk1_ref, lq2_ref, lk2_ref, g_ref, o_ref, *, lam_init):
    tq, tk, dh = ATTN_TQ, ATTN_TK, DA_HEAD_DIM
    qi = pl.program_id(2)
    q = q_ref[...]
    lane = lax.broadcasted_iota(jnp.int32, q.shape, 1)
    zero = jnp.zeros_like(q)
    qs = (jnp.where(lane < dh, q, zero), jnp.where(lane >= dh, q, zero))

    sw = ATTN_Q_STRIPE
    chains = [(c, r) for r in range(tq // sw) for c in range(2)]
    q_parts = [qs[c][r * sw:(r + 1) * sw, :] for c, r in chains]

    def absorb(blocks, carry):
        def block_scores(j, masked):
            kb = k_ref[pl.ds(pl.multiple_of(j * tk, tk), tk), :]
            out = []
            for (c, r), qp in zip(chains, q_parts):
                nk = min(tk, (r + 1) * sw) if masked else tk
                out.append(lax.dot_general(kb[0:nk], qp, (((1,), (1,)), ((), ())), preferred_element_type=F32))
            return out

        scores = [block_scores(*blocks[0])]
        for i, (j, masked) in enumerate(blocks):
            if i + 1 < len(blocks):
                scores.append(block_scores(*blocks[i + 1]))
            vt = jnp.concatenate([vt_ref[j], ones], axis=0)
            out = []
            for (c, r), s, (m, acc) in zip(chains, scores[i], carry):
                nk = s.shape[0]
                if masked:
                    keys = lax.broadcasted_iota(jnp.int32, (nk, sw), 0)
                    queries = lax.broadcasted_iota(jnp.int32, (nk, sw), 1) + r * sw
                    s = jnp.where(keys <= queries, s, NEG_BIG)
                m_new = jnp.maximum(m, jnp.max(s, axis=0, keepdims=True))
                alpha = jnp.exp2(m - m_new)
                p = jnp.exp2(s - m_new)
                acc = alpha * acc + jnp.dot(vt[:, 0:nk], p.astype(BF16), preferred_element_type=F32)
                out.append((m_new, acc))
            carry = tuple(out)
        return carry

    def finish(carry):
        a0, a1 = (jnp.concatenate([carry[i][1] for i, (c, _) in enumerate(chains) if c == cc], axis=1)
                  for cc in range(2))
        l0, l1 = a0[2 * dh:2 * dh + 1], a1[2 * dh:2 * dh + 1]
        lam = _diff_lambda(lq1_ref, lk1_ref, lq2_ref, lk2_ref, lam_init)
        o = (a0[0:2 * dh] * (1.0 / l0) - lam * (a1[0:2 * dh] * (1.0 / l1))).T
        o_ref[...] = (_rms_norm(o, g_ref[...]) * (1.0 - lam_init)).astype(o_ref.dtype)

    n = ATTN_BLOCKS_PER_TRIP
    ones = jnp.ones((ATTN_ONES_ROWS, tk), BF16)
    init = tuple((jnp.full((1, sw), NEG_BIG, F32), jnp.zeros((2 * dh + ATTN_ONES_ROWS, sw), F32))
                 for _ in chains)
    carry = lax.fori_loop(0, qi // n, lambda p, c: absorb([(n * p + i, False) for i in range(n)], c), init)

    for rem in range(n):
        @pl.when(qi % n == rem)
        def _(rem=rem):
            finish(absorb([(qi - rem + i, False) for i in range(rem)] + [(qi, True)], carry))


def _flash(qb, kb, vt, lq1, lk1, lq2, lk2, subln_w, lam_init):
    assert ATTN_TQ == ATTN_TK == ROW_TILE
    bsz, seq, width = qb.shape
    hw = 2 * DA_HEAD_DIM
    nk = seq // ATTN_TK
    vec = lambda a: a.reshape(1, -1)
    kv = pl.BlockSpec((None, seq, hw), lambda b, h, i: (b, 0, h))
    vts = pl.BlockSpec((nk, hw, ATTN_TK), lambda b, h, i: (b, h, 0))
    qo = pl.BlockSpec((None, ATTN_TQ, hw), lambda b, h, i: (b, i, h))
    small = _resident((1, DA_HEAD_DIM))
    return pl.pallas_call(
        functools.partial(_flash_kernel, lam_init=lam_init),
        grid=(bsz, width // hw, seq // ATTN_TQ),
        in_specs=[qo, kv, vts, small, small, small, small, _resident((1, hw))],
        out_specs=qo,
        out_shape=jax.ShapeDtypeStruct((bsz, seq, width), BF16),
        compiler_params=_params("parallel", "parallel", "arbitrary"),
    )(qb, kb, vt, vec(lq1), vec(lk1), vec(lq2), vec(lk2), vec(subln_w))


def _decode_attn_kernel(pt_ref, q_ref, kn_ref, vn_ref, lq1_ref, lk1_ref, lq2_ref, lk2_ref, g_ref, *rest,
                        lam_init, n_pages):
    k_refs = rest[0:n_pages]
    v_refs = rest[n_pages:2 * n_pages]
    o_ref, m_ref, l_ref, acc_ref = rest[2 * n_pages:]
    b, s = pl.program_id(0), pl.program_id(1)
    width = q_ref.shape[1]
    n_sub = width // DA_HEAD_DIM
    hw = 2 * DA_HEAD_DIM
    rows = lax.broadcasted_iota(jnp.int32, (n_sub, width), 0)
    lanes = lax.broadcasted_iota(jnp.int32, (n_sub, width), 1)
    qrow = q_ref[pl.ds(b, 1), :]
    qm = jnp.where(lanes // DA_HEAD_DIM == rows, jnp.broadcast_to(qrow, (n_sub, width)), 0.0).astype(BF16)

    @pl.when(s == 0)
    def _():
        kn = jnp.broadcast_to(kn_ref[pl.ds(b, 1), :], (SUBLANES, width))
        m_ref[...] = _dot_nt(qm, kn)[:, 0:1]
        l_ref[...] = jnp.ones_like(l_ref)
        acc_ref[...] = jnp.broadcast_to(vn_ref[pl.ds(b, 1), :], (n_sub, width))

    n_heads = width // hw
    page = k_refs[0].shape[0] // n_heads
    def by_position(ref):
        return jnp.concatenate([ref[pl.ds(h, page, stride=n_heads), :].astype(BF16) for h in range(n_heads)], axis=1)

    scores = [_dot_nt(qm, by_position(k_ref)) for k_ref in k_refs]
    m_old = m_ref[...]
    m_new = m_old
    for sc in scores:
        m_new = jnp.maximum(m_new, jnp.max(sc, axis=-1, keepdims=True))
    alpha = jnp.exp2(m_old - m_new)
    l = alpha * l_ref[...]
    acc = alpha * acc_ref[...]
    for sc, v_ref in zip(scores, v_refs):
        p = jnp.exp2(sc - m_new)
        l = l + jnp.sum(p, axis=-1, keepdims=True)
        acc = acc + _dot(p, by_position(v_ref))
    acc_ref[...] = acc
    m_ref[...] = m_new
    l_ref[...] = l

    @pl.when(s == pl.num_programs(1) - 1)
    def _():
        lam = _diff_lambda(lq1_ref, lk1_ref, lq2_ref, lk2_ref, lam_init)
        o = acc_ref[...] / l
        own = lanes // hw == rows // 2
        coef = jnp.where(rows % 2 == 0, 1.0, -lam)
        out = jnp.sum(jnp.where(own, coef * o, 0.0), axis=0, keepdims=True)
        g = g_ref[...]
        parts = [_rms_norm(out[:, h * hw:(h + 1) * hw], g) for h in range(width // hw)]
        o_ref[...] = jnp.concatenate(parts, axis=1) * (1.0 - lam_init)


def _decode_attn(page_table, q, k_new, v_new, cache_k, cache_v, layer, lq1, lk1, lq2, lk2, subln_w, lam_init):
    bsz, width = q.shape
    n_tab = page_table.shape[1]
    page_rows = cache_k.shape[2]
    npg = DEC_PAGES_PER_STEP
    assert n_tab % npg == 0
    hw = 2 * DA_HEAD_DIM
    vec = lambda a: a.reshape(1, -1)
    res = lambda shape: pl.BlockSpec(shape, lambda b, s, pt: (0,) * len(shape))

    def page_spec(i):
        return pl.BlockSpec((None, None, page_rows, hw),
                            lambda b, s, pt: (layer, pt[b * n_tab + s * npg + i], 0, 0))

    pages = [page_spec(i) for i in range(npg)]
    n_sub = width // DA_HEAD_DIM
    grid_spec = pltpu.PrefetchScalarGridSpec(
        num_scalar_prefetch=1,
        grid=(bsz, n_tab // npg),
        in_specs=[res((bsz, width))] * 3 + [res((1, DA_HEAD_DIM))] * 4 + [res((1, hw))] + pages + pages,
        out_specs=pl.BlockSpec((None, 1, width), lambda b, s, pt: (b, 0, 0)),
        scratch_shapes=[pltpu.VMEM((n_sub, 1), F32), pltpu.VMEM((n_sub, 1), F32), pltpu.VMEM((n_sub, width), F32)],
    )
    out = pl.pallas_call(
        functools.partial(_decode_attn_kernel, lam_init=lam_init, n_pages=npg),
        grid_spec=grid_spec,
        out_shape=jax.ShapeDtypeStruct((bsz, 1, width), F32),
        compiler_params=_params("arbitrary", "arbitrary"),
    )(page_table.reshape(-1), q, k_new, v_new, vec(lq1), vec(lk1), vec(lq2), vec(lk2), vec(subln_w),
      *([cache_k] * npg), *([cache_v] * npg))
    return out.reshape(bsz, width)


def kernel(x_prompt, x_sample, state_s5_re, state_s5_im, state_ssd, state_conv, cache_k, cache_v, page_table, p_prompt, p_sample, ln1_g, ln1_b, ln2_g, ln2_b, ssm_w_in, s5_lam_re, s5_lam_im, s5_log_step, s5_b_re, s5_b_im, s5_c_re, s5_c_im, s5_d, s5_w_glu, s5_b_glu, ssd_conv_w, ssd_conv_b, ssd_dt_bias, ssd_a_log, ssd_d, ssd_norm_w, ssm_w_out, attn_w_qkv, attn_lq1, attn_lk1, attn_lq2, attn_lk2, attn_subln_w, attn_w_out, mlp_w1, mlp_w2, ple_w_proj, ple_w_gate):
    bp, seq_p, d_model = x_prompt.shape
    bs, seq_s, _ = x_sample.shape
    assert seq_s == 1
    depth = ln1_g.shape[0]
    alpha = (2 * depth) ** 0.25
    s5_w = s5_d.shape[1]
    s5_groups, s5_state = s5_lam_re.shape[1:]
    ns = s5_groups * s5_state
    ssd_heads = ssd_dt_bias.shape[1]
    ssd_w = ssd_heads * SSD_HEAD_DIM
    cdim = ssd_conv_w.shape[2]
    kc = ssd_conv_w.shape[1]
    n_state = state_ssd.shape[-1]
    da_w = attn_w_out.shape[1]
    da_heads = da_w // (2 * DA_HEAD_DIM)
    page = cache_k.shape[2]
    past_len = page_table.shape[1] * page
    tp = bp * seq_p

    xp = x_prompt.reshape(tp, d_model)
    xs = x_sample.reshape(bs, d_model)
    tables_p = _rope_tables(jnp.arange(seq_p))
    tables_s = _rope_tables(jnp.full((bs,), past_len))
    ck = cache_k.reshape(cache_k.shape[0], cache_k.shape[1], page * da_heads, 2 * DA_HEAD_DIM)
    cv = cache_v.reshape(cache_v.shape[0], cache_v.shape[1], page * da_heads, 2 * DA_HEAD_DIM)

    outs = {name: [] for name in ("s5p", "s5s", "ssdp", "ssds", "convp", "convs", "kp", "vp", "ks", "vs")}
    for i in range(depth):
        j = i // 2
        if i % 2 == 0:
            n_main = s5_w + ssd_w + cdim
            w_in = [ssm_w_in[j][:, :n_main].astype(BF16),
                    jnp.pad(ssm_w_in[j][:, n_main:], ((0, 0), (0, LANES - ssd_heads))).astype(BF16)]
            splits = ((s5_w, ssd_w, cdim), (LANES,))
            a, bmat, cmat = _s5_prep(s5_lam_re[j], s5_lam_im[j], s5_log_step[j], s5_b_re[j], s5_b_im[j],
                                     s5_c_re[j], s5_c_im[j])
            wg = s5_w_glu[j].astype(BF16)
            ssd_prm = (ssd_conv_w[j], ssd_conv_b[j], ssd_dt_bias[j], ssd_a_log[j], ssd_d[j], ssd_norm_w[j])

            u, z, xbc, dt = _proj(xp, w_in, splits)
            ya, hfin = _s5_scan(u.reshape(bp, seq_p, s5_w), jnp.zeros((bp, 1, 2 * ns), F32), a, bmat, cmat,
                                s5_d[j], wg, s5_b_glu[j])
            yb, conv_p, st_p = _ssd_scan(z.reshape(bp, seq_p, ssd_w), xbc.reshape(bp, seq_p, cdim),
                                         dt.reshape(bp, seq_p, LANES), jnp.zeros((bp, kc - 1, cdim), F32),
                                         jnp.zeros((bp, ssd_w, n_state), F32), *ssd_prm)
            mixes_p = [ya.reshape(tp, s5_w), yb.reshape(tp, ssd_w)]
            outs["s5p"].append(hfin.reshape(bp, 2, s5_groups, s5_state))
            outs["ssdp"].append(st_p.reshape(bp, ssd_heads, SSD_HEAD_DIM, n_state))
            outs["convp"].append(conv_p)

            u, z, xbc, dt = _proj(xs, w_in, splits)
            h0 = jnp.concatenate([state_s5_re[j].reshape(bs, ns), state_s5_im[j].reshape(bs, ns)], axis=1)
            ya, hnew = _s5_step(u, h0, a, bmat, cmat, s5_d[j], wg, s5_b_glu[j])
            yb, conv_s, st_s = _ssd_step(z, xbc, dt, state_conv[j], state_ssd[j].reshape(bs, ssd_w, n_state),
                                         *ssd_prm)
            mixes_s = [ya, yb]
            outs["s5s"].append(hnew.reshape(bs, 2, s5_groups, s5_state))
            outs["ssds"].append(st_s.reshape(bs, ssd_heads, SSD_HEAD_DIM, n_state))
            outs["convs"].append(conv_s)
            w_out = ssm_w_out[j].astype(BF16)
            w_outs = [w_out[0:s5_w], w_out[s5_w:]]
        else:
            lam_init = 0.8 - 0.6 * math.exp(-0.3 * i)
            w_qkv = attn_w_qkv[j].astype(BF16)
            lam_prm = (attn_lq1[j], attn_lk1[j], attn_lq2[j], attn_lk2[j], attn_subln_w[j])

            k, v, qb, kb, vt = _qkv(xp, w_qkv, tables_p, True)
            shape3 = (bp, seq_p, da_w)
            o = _flash(qb.reshape(shape3), kb.reshape(shape3), vt, *lam_prm, lam_init)
            mixes_p = [o.reshape(tp, da_w)]
            outs["kp"].append(k.reshape(bp, seq_p, da_heads, 2 * DA_HEAD_DIM))
            outs["vp"].append(v.reshape(bp, seq_p, da_heads, 2 * DA_HEAD_DIM))

            k, v, qf = _qkv(xs, w_qkv, tables_s, False)
            o = _decode_attn(page_table, qf, k.reshape(bs, da_w), v.reshape(bs, da_w), ck, cv, j, *lam_prm,
                             lam_init)
            mixes_s = [o]
            outs["ks"].append(k.reshape(bs, 1, da_heads, 2 * DA_HEAD_DIM))
            outs["vs"].append(v.reshape(bs, 1, da_heads, 2 * DA_HEAD_DIM))
            w_outs = [attn_w_out[j].astype(BF16)]

        tail = (ln1_g[i], ln1_b[i], ln2_g[i], ln2_b[i], mlp_w1[i].astype(BF16), mlp_w2[i].astype(BF16),
                ple_w_proj[i].astype(BF16), ple_w_gate[i].astype(BF16), alpha)
        xp = _finish(xp, mixes_p, w_outs, p_prompt[i].reshape(tp, -1), *tail)
        xs = _finish(xs, mixes_s, w_outs, p_sample[i].reshape(bs, -1), *tail)

    s5p, s5s = jnp.stack(outs["s5p"]), jnp.stack(outs["s5s"])
    return (xp.reshape(bp, seq_p, d_model), xs.reshape(bs, 1, d_model),
            s5p[:, :, 0], s5p[:, :, 1], s5s[:, :, 0], s5s[:, :, 1],
            jnp.stack(outs["ssdp"]), jnp.stack(outs["ssds"]), jnp.stack(outs["convp"]), jnp.stack(outs["convs"]),
            jnp.stack(outs["kp"]), jnp.stack(outs["vp"]), jnp.stack(outs["ks"]), jnp.stack(outs["vs"]))
```
